```python
import jax, jax.numpy as jnp
from jax import lax
import numpy as np

D_MODEL = 1024
BATCH = 16
SEQ = 2048
DEPTH = 2
DEC_BATCH = 32
DEC_SEQ = 8
PAST_LEN = 16384
PAGE_SIZE = 128

RET_HEADS = 4
RET_DK = 128
RET_DV = 128
RET_CHUNK = 128
ATT_HEADS = 8
ATT_HD = 64
MOBA_BLOCK = 256
MOBA_TOPK = 3
Q_BLOCK = 128
ROPE_THETA = 10000.0
RMS_EPS = 1e-6
D_RET_K = RET_HEADS * RET_DK
D_RET = RET_HEADS * RET_DV
D_ATT = ATT_HEADS * ATT_HD
D_MIX = D_RET + D_ATT
D_IN = 2 * D_RET_K + 2 * D_RET + 3 * D_ATT
D_FF = -(-8 * D_MODEL // (3 * 256)) * 256

kernel_name = 'hymba_retention_moba_step'


def _rms_norm(x, g):
    xf = x.astype(jnp.float32)
    y = xf * lax.rsqrt(jnp.mean(xf * xf, axis=-1, keepdims=True) + RMS_EPS)
    return (y * g.astype(jnp.float32)).astype(x.dtype)


def _rotary(x, pos, inv_freq):
    ang = pos.astype(jnp.float32)[:, None] * inv_freq[None, :]
    cos = jnp.cos(ang)[:, None, :]
    sin = jnp.sin(ang)[:, None, :]
    x1, x2 = jnp.split(x.astype(jnp.float32), 2, axis=-1)
    return jnp.concatenate([x1 * cos - x2 * sin, x2 * cos + x1 * sin], axis=-1)


def _log_gamma():
    return jnp.log1p(-jnp.exp2(-5.0 - jnp.arange(RET_HEADS, dtype=jnp.float32)))


def _project(xn, w_in_l, qn_g, kn_g, pos):
    B, T, _ = xn.shape
    sizes = (D_RET_K, D_RET_K, D_RET, D_RET, D_ATT, D_ATT, D_ATT)
    cuts = [sum(sizes[:i + 1]) for i in range(len(sizes) - 1)]
    rq, rk, rv, rg, aq, ak, av = jnp.split(xn @ w_in_l, cuts, axis=-1)
    ret_freq = 1.0 / (ROPE_THETA ** jnp.linspace(0.0, 1.0, RET_DK // 2, dtype=jnp.float32))
    att_freq = 1.0 / (ROPE_THETA ** (jnp.arange(0, ATT_HD, 2, dtype=jnp.float32) / ATT_HD))
    rq = _rotary(rq.reshape(B, T, RET_HEADS, RET_DK), pos, ret_freq)
    rk = _rotary(rk.reshape(B, T, RET_HEADS, RET_DK), pos, ret_freq) * RET_DK ** -0.5
    rv = rv.reshape(B, T, RET_HEADS, RET_DV).astype(jnp.float32)
    aq = _rotary(_rms_norm(aq.reshape(B, T, ATT_HEADS, ATT_HD), qn_g), pos, att_freq).astype(xn.dtype)
    ak = _rotary(_rms_norm(ak.reshape(B, T, ATT_HEADS, ATT_HD), kn_g), pos, att_freq).astype(xn.dtype)
    av = av.reshape(B, T, ATT_HEADS, ATT_HD)
    return rq, rk, rv, rg, aq, ak, av


def _retention_chunk(q, k, v, state, log_gamma):
    L = q.shape[1]
    i = jnp.arange(L, dtype=jnp.float32)
    diff = i[:, None] - i[None, :]
    decay = jnp.where(diff >= 0, jnp.exp(log_gamma[:, None, None] * jnp.maximum(diff, 0.0)), 0.0)
    read = jnp.exp(log_gamma[None, :] * (i[:, None] + 1.0))
    write = jnp.exp(log_gamma[None, :] * (L - 1.0 - i[:, None]))
    s = jnp.einsum('bihd,bjhd->bhij', q, k) * decay
    o = jnp.einsum('bhij,bjhe->bihe', s, v) + jnp.einsum('bihd,bhde->bihe', q, state) * read[None, :, :, None]
    new_state = (jnp.exp(log_gamma * L)[None, :, None, None] * state
                 + jnp.einsum('bjhd,bjhe->bhde', k * write[None, :, :, None], v))
    return o, new_state


def _retention_prompt(q, k, v, log_gamma):
    B, S, H, Dk = q.shape
    nc = S // RET_CHUNK

    def chunks(t):
        return t.reshape(B, nc, RET_CHUNK, H, t.shape[-1]).swapaxes(0, 1)

    def step(st, qkv):
        o, st = _retention_chunk(qkv[0], qkv[1], qkv[2], st, log_gamma)
        return st, o

    st0 = jnp.zeros((B, H, Dk, v.shape[-1]), jnp.float32)
    st, o = lax.scan(step, st0, (chunks(q), chunks(k), chunks(v)))
    return o.swapaxes(0, 1).reshape(B, S, H, v.shape[-1]), st


def _softmax_attend(scores, vals):
    sizes = [s.shape[-1] for s in scores]
    p = jax.nn.softmax(jnp.concatenate(scores, axis=-1), axis=-1)
    out, off = 0.0, 0
    for (eq, val), n in zip(vals, sizes):
        out = out + jnp.einsum(eq, p[..., off:off + n], val.astype(jnp.float32))
        off += n
    return out


def _moba_prompt(q, k, v):
    B, S, H, Dh = q.shape
    nb = -(-S // MOBA_BLOCK)
    pad = nb * MOBA_BLOCK - S
    nqb = S // Q_BLOCK
    topk = min(MOBA_TOPK, nb - 1)
    scale = Dh ** -0.5

    def blocks(t):
        t = jnp.pad(t, ((0, 0), (0, pad), (0, 0), (0, 0)))
        return t.reshape(B, nb, MOBA_BLOCK, H, Dh).transpose(0, 3, 1, 2, 4)

    kb, vb = blocks(k), blocks(v)
    qf = q.astype(jnp.float32)
    qb = qf.reshape(B, nqb, Q_BLOCK, H, Dh).transpose(0, 1, 3, 2, 4).reshape(B * nqb, H, Q_BLOCK, Dh)
    xs = [jnp.repeat(jnp.arange(B), nqb), jnp.tile(jnp.arange(nqb), B), qb]
    if topk > 0:
        kmean = jnp.mean(kb.astype(jnp.float32), axis=3)
        own = jnp.arange(S) // MOBA_BLOCK
        gate = jnp.einsum('bshd,bhnd->bhsn', qf, kmean)
        gate = jnp.where(jnp.arange(nb)[None, :] < own[:, None], gate, -jnp.inf)
        gval, gidx = lax.top_k(gate, topk)

        def per_qblock(t):
            return t.reshape(B, H, nqb, Q_BLOCK, topk).transpose(0, 2, 1, 3, 4).reshape(B * nqb, H, Q_BLOCK, topk)

        xs += [per_qblock(gidx), per_qblock(jnp.isfinite(gval))]
    hsel = jnp.arange(H)[:, None, None]
    qoff = jnp.arange(Q_BLOCK)
    koff = jnp.arange(MOBA_BLOCK)

    def one(args):
        bi, ci, qc = args[0], args[1], args[2]
        kh, vh = kb[bi], vb[bi]
        ob = (ci * Q_BLOCK) // MOBA_BLOCK
        s_own = jnp.einsum('hqd,hpd->hqp', qc, kh[:, ob].astype(jnp.float32)) * scale
        causal = ob * MOBA_BLOCK + koff[None, :] <= ci * Q_BLOCK + qoff[:, None]
        scores = [jnp.where(causal[None], s_own, -jnp.inf)]
        vals = [('hqp,hpd->qhd', vh[:, ob])]
        if topk > 0:
            idx, ok = args[3], args[4]
            kg = kh[hsel, idx].reshape(H, Q_BLOCK, topk * MOBA_BLOCK, Dh)
            vg = vh[hsel, idx].reshape(H, Q_BLOCK, topk * MOBA_BLOCK, Dh)
            s_sel = jnp.einsum('hqd,hqnd->hqn', qc, kg.astype(jnp.float32)) * scale
            ok_n = jnp.repeat(ok, MOBA_BLOCK, axis=-1)
            scores.append(jnp.where(ok_n, s_sel, -jnp.inf))
            vals.append(('hqn,hqnd->qhd', vg))
        return _softmax_attend(scores, vals)

    o = lax.map(one, tuple(xs))
    return o.reshape(B, S, H, Dh)


def _moba_sample(q, k, v, cache_k_l, cache_v_l, page_table):
    DB, L, H, Dh = q.shape
    n_pages = page_table.shape[1]
    past = n_pages * PAGE_SIZE
    ppb = MOBA_BLOCK // PAGE_SIZE
    nbf = past // MOBA_BLOCK
    own_start = nbf * MOBA_BLOCK
    n_own_pages = n_pages - own_start // PAGE_SIZE
    scale = Dh ** -0.5
    qf = q.astype(jnp.float32).transpose(0, 2, 1, 3)
    causal = jnp.arange(L)[None, :] <= jnp.arange(L)[:, None]
    s_new = jnp.einsum('bhld,bmhd->bhlm', qf, k.astype(jnp.float32)) * scale
    scores = [jnp.where(causal, s_new, -jnp.inf)]
    vals = [('bhlm,bmhd->bhld', v)]
    if n_own_pages > 0:
        own_pages = page_table[:, own_start // PAGE_SIZE:]
        r = n_own_pages * PAGE_SIZE
        ko = cache_k_l[own_pages].reshape(DB, r, H, Dh)
        vo = cache_v_l[own_pages].reshape(DB, r, H, Dh)
        scores.append(jnp.einsum('bhld,brhd->bhlr', qf, ko.astype(jnp.float32)) * scale)
        vals.append(('bhlr,brhd->bhld', vo))
    topk = min(MOBA_TOPK, nbf)
    if topk > 0:
        k_full = cache_k_l[page_table[:, :nbf * ppb]].reshape(DB, nbf, MOBA_BLOCK, H, Dh)
        kmean = jnp.mean(k_full.astype(jnp.float32), axis=2)
        gate = jnp.einsum('bhld,bnhd->bhln', qf, kmean)
        _, gidx = lax.top_k(gate, topk)
        pages = page_table[jnp.arange(DB)[:, None, None, None, None], gidx[..., None] * ppb + jnp.arange(ppb)]
        hsel = jnp.arange(H)[None, :, None, None, None]
        kg = cache_k_l[pages, :, hsel].reshape(DB, H, L, topk * MOBA_BLOCK, Dh)
        vg = cache_v_l[pages, :, hsel].reshape(DB, H, L, topk * MOBA_BLOCK, Dh)
        scores.append(jnp.einsum('bhld,bhlnd->bhln', qf, kg.astype(jnp.float32)) * scale)
        vals.append(('bhln,bhlnd->bhld', vg))
    o = _softmax_attend(scores, vals)
    return o.transpose(0, 2, 1, 3)


def _mix_out(h, ret_o, ret_g, att_o, ret_norm_g_l, w_out_l):
    B, T, _ = h.shape
    y_ret = _rms_norm(ret_o, ret_norm_g_l).reshape(B, T, D_RET) * jax.nn.silu(ret_g.astype(jnp.float32))
    y = jnp.concatenate([y_ret.astype(h.dtype), att_o.reshape(B, T, D_ATT).astype(h.dtype)], axis=-1)
    return h + y @ w_out_l


def _ffn(h, g, wg, wu, wd):
    hn = _rms_norm(h, g)
    return h + (jax.nn.silu(hn @ wg) * (hn @ wu)) @ wd


def setup_inputs(seed: int = 0) -> dict:
    key = jax.random.key(seed)
    ks = jax.random.split(key, 20)
    f32 = jnp.float32
    n_pages = PAST_LEN // PAGE_SIZE
    n_pool = -(-5 * DEC_BATCH * n_pages // 4)

    def gain(k, shape):
        return 1.0 + 0.02 * jax.random.normal(k, shape, f32)

    def dense(k, shape, fan_in):
        return jax.random.normal(k, shape, f32) * fan_in ** -0.5

    page_table = jax.random.permutation(ks[5], n_pool)[:DEC_BATCH * n_pages].reshape(DEC_BATCH, n_pages).astype(jnp.int32)
    return {
        'x_prompt': jax.random.normal(ks[0], (BATCH, SEQ, D_MODEL), f32),
        'x_sample': jax.random.normal(ks[1], (DEC_BATCH, DEC_SEQ, D_MODEL), f32),
        'cache_k': jax.random.normal(ks[2], (DEPTH, n_pool, PAGE_SIZE, ATT_HEADS, ATT_HD), f32),
        'cache_v': jax.random.normal(ks[3], (DEPTH, n_pool, PAGE_SIZE, ATT_HEADS, ATT_HD), f32),
        'state_ret': 0.5 * jax.random.normal(ks[4], (DEPTH, DEC_BATCH, RET_HEADS, RET_DK, RET_DV), f32),
        'page_table': page_table,
        'norm1_g': gain(ks[6], (DEPTH, D_MODEL)),
        'w_in': dense(ks[7], (DEPTH, D_MODEL, D_IN), D_MODEL),
        'q_norm_g': gain(ks[8], (DEPTH, ATT_HD)),
        'k_norm_g': gain(ks[9], (DEPTH, ATT_HD)),
        'ret_norm_g': gain(ks[10], (DEPTH, RET_HEADS, RET_DV)),
        'w_out': dense(ks[11], (DEPTH, D_MIX, D_MODEL), D_MIX),
        'norm2_g': gain(ks[12], (DEPTH, D_MODEL)),
        'w_gate': dense(ks[13], (DEPTH, D_MODEL, D_FF), D_MODEL),
        'w_up': dense(ks[14], (DEPTH, D_MODEL, D_FF), D_MODEL),
        'w_down': dense(ks[15], (DEPTH, D_FF, D_MODEL), D_FF),
    }


def reference(x_prompt, x_sample, cache_k, cache_v, state_ret, page_table, norm1_g, w_in, q_norm_g, k_norm_g,
              ret_norm_g, w_out, norm2_g, w_gate, w_up, w_down):
    S = x_prompt.shape[1]
    L = x_sample.shape[1]
    past = page_table.shape[1] * PAGE_SIZE
    pos_p = jnp.arange(S)
    pos_s = past + jnp.arange(L)
    log_gamma = _log_gamma()
    hp, hs = x_prompt, x_sample
    kp, vp, rp, kss, vss, rss = [], [], [], [], [], []
    for l in range(DEPTH):
        rq, rk, rv, rg, aq, ak, av = _project(_rms_norm(hp, norm1_g[l]), w_in[l], q_norm_g[l], k_norm_g[l], pos_p)
        ro, rst = _retention_prompt(rq, rk, rv, log_gamma)
        ao = _moba_prompt(aq, ak, av)
        hp = _mix_out(hp, ro, rg, ao, ret_norm_g[l], w_out[l])
        hp = _ffn(hp, norm2_g[l], w_gate[l], w_up[l], w_down[l])
        kp.append(ak)
        vp.append(av)
        rp.append(rst)
        rq, rk, rv, rg, aq, ak, av = _project(_rms_norm(hs, norm1_g[l]), w_in[l], q_norm_g[l], k_norm_g[l], pos_s)
        ro, rst = _retention_chunk(rq, rk, rv, state_ret[l].astype(jnp.float32), log_gamma)
        ao = _moba_sample(aq, ak, av, cache_k[l], cache_v[l], page_table)
        hs = _mix_out(hs, ro, rg, ao, ret_norm_g[l], w_out[l])
        hs = _ffn(hs, norm2_g[l], w_gate[l], w_up[l], w_down[l])
        kss.append(ak)
        vss.append(av)
        rss.append(rst)
    k_prompt = jnp.stack(kp)
    v_prompt = jnp.stack(vp)
    ret_prompt = jnp.stack(rp)
    k_sample = jnp.stack(kss)
    v_sample = jnp.stack(vss)
    ret_sample = jnp.stack(rss)
    return (hp, hs, k_prompt, v_prompt, ret_prompt, k_sample, v_sample, ret_sample)
```

```python
import functools

import jax
import jax.numpy as jnp
from jax import lax
from jax.experimental import pallas as pl
from jax.experimental.pallas import tpu as pltpu

F32 = jnp.float32
BF16 = jnp.bfloat16

PAGE_SIZE = 128
RET_HEADS = 4
RET_DK = 128
RET_DV = 128
RET_CHUNK = 128
ATT_HEADS = 8
ATT_HD = 64
MOBA_BLOCK = 256
MOBA_TOPK = 3
Q_BLOCK = 128
ROPE_THETA = 10000.0
RMS_EPS = 1e-6
D_RET = RET_HEADS * RET_DV
D_ATT = ATT_HEADS * ATT_HD
PAGES_PER_BLOCK = MOBA_BLOCK // PAGE_SIZE

VMEM_LIMIT_BYTES = 56 * 1024 * 1024
NEG_INF = float("-inf")

_NT = (((1,), (1,)), ((), ()))
_TN = (((0,), (0,)), ((), ()))


def _params(*sem):
    return pltpu.CompilerParams(dimension_semantics=sem, vmem_limit_bytes=VMEM_LIMIT_BYTES)


def _const_spec(shape):
    nd = len(shape)
    return pl.BlockSpec(shape, lambda *_: (0,) * nd, pipeline_mode=pl.Buffered(1))


def _inproj_kernel(x_ref, g_ref, ws_ref, wt_ref, rcos_ref, rsin_ref, acos_ref, asin_ref, qg_ref, kg_ref,
                   rq_ref, rk_ref, rv_ref, rg_ref, q_ref, qt_ref, kt_ref, vt_ref):
    x = x_ref[...]
    xn = (x * lax.rsqrt(jnp.mean(x * x, axis=-1, keepdims=True) + RMS_EPS) * g_ref[...]).astype(BF16)
    tm = x.shape[0]

    def std(c):
        return jnp.dot(xn, ws_ref[:, c * D_RET:(c + 1) * D_RET], preferred_element_type=F32)

    rcos = rcos_ref[...]
    rsin = rsin_ref[...]
    for c, o_ref, scale in ((0, rq_ref, None), (1, rk_ref, RET_DK ** -0.5)):
        acc = std(c)
        for h in range(RET_HEADS):
            xh = acc[:, h * RET_DK:(h + 1) * RET_DK]
            r = xh * rcos + pltpu.roll(xh, RET_DK // 2, 1) * rsin
            o_ref[:, h * RET_DK:(h + 1) * RET_DK] = r if scale is None else r * scale
    rv_ref[...] = std(2)
    rg_ref[...] = std(3)

    def tr(c):
        return lax.dot_general(wt_ref[c * D_ATT:(c + 1) * D_ATT, :], xn, _NT, preferred_element_type=F32)

    acos = acos_ref[...]
    asin = asin_ref[...]
    half = ATT_HD // 2
    for c, gn_ref, o_ref in ((0, qg_ref, qt_ref), (1, kg_ref, kt_ref)):
        acc = tr(c)
        for h in range(ATT_HEADS):
            xh = acc[h * ATT_HD:(h + 1) * ATT_HD, :]
            y = xh * lax.rsqrt(jnp.mean(xh * xh, axis=0, keepdims=True) + RMS_EPS) * gn_ref[...]
            y1 = y[:half]
            y2 = y[half:]
            o_ref[h, 0:half, :] = y1 * acos - y2 * asin
            o_ref[h, half:ATT_HD, :] = y2 * acos + y1 * asin
    q_ref[...] = qt_ref[...].reshape(D_ATT, tm).T
    vt_ref[...] = tr(2).reshape(ATT_HEADS, ATT_HD, tm)


def _inproj(x, g, ws, wt, tabs, qg, kg, nb, s_len, tm):
    m, d = x.shape
    nt = s_len // tm
    rcos, rsin, acos_t, asin_t = tabs
    row = lambda i: (i, 0)
    row_out = pl.BlockSpec((tm, D_RET), row)
    t_out = pl.BlockSpec((None, ATT_HEADS, ATT_HD, tm), lambda i: (i // nt, 0, 0, i % nt))
    t_shape = jax.ShapeDtypeStruct((nb, ATT_HEADS, ATT_HD, s_len), F32)
    r_shape = jax.ShapeDtypeStruct((m, D_RET), F32)
    return pl.pallas_call(
        _inproj_kernel,
        grid=(m // tm,),
        in_specs=[
            pl.BlockSpec((tm, d), row),
            _const_spec((1, d)),
            _const_spec(ws.shape),
            _const_spec(wt.shape),
            pl.BlockSpec((tm, RET_DK), lambda i: (i % nt, 0)),
            pl.BlockSpec((tm, RET_DK), lambda i: (i % nt, 0)),
            pl.BlockSpec((ATT_HD // 2, tm), lambda i: (0, i % nt)),
            pl.BlockSpec((ATT_HD // 2, tm), lambda i: (0, i % nt)),
            _const_spec((ATT_HD, 1)),
            _const_spec((ATT_HD, 1)),
        ],
        out_specs=[row_out, row_out, row_out, row_out, row_out, t_out, t_out, t_out],
        out_shape=[r_shape, r_shape, r_shape, r_shape, r_shape, t_shape, t_shape, t_shape],
        compiler_params=_params("arbitrary"),
        name="inproj",
    )(x, g, ws, wt, rcos, rsin, acos_t, asin_t, qg, kg)


def _retention_kernel(rq_ref, rk_ref, rv_ref, rg_ref, s0_ref, decay_ref, read_ref, write_ref, gl_ref, ng_ref,
                      y_ref, so_ref, st_ref):
    c = pl.program_id(1)

    @pl.when(c == 0)
    def _():
        st_ref[...] = s0_ref[...]

    for h in range(RET_HEADS):
        sl = slice(h * RET_DK, (h + 1) * RET_DK)
        q = rq_ref[:, sl].astype(BF16)
        k = rk_ref[:, sl]
        v = rv_ref[:, sl].astype(BF16)
        state = st_ref[h]
        s = lax.dot_general(q, k.astype(BF16), _NT, preferred_element_type=F32) * decay_ref[h]
        o = (jnp.dot(s.astype(BF16), v, preferred_element_type=F32)
             + jnp.dot(q, state.astype(BF16), preferred_element_type=F32) * read_ref[:, sl])
        kw = (k * write_ref[:, sl]).astype(BF16)
        st_ref[h] = gl_ref[:, sl] * state + lax.dot_general(kw, v, _TN, preferred_element_type=F32)
        y = o * lax.rsqrt(jnp.mean(o * o, axis=-1, keepdims=True) + RMS_EPS) * ng_ref[:, sl]
        g = rg_ref[:, sl]
        y_ref[:, sl] = y * (g * jax.nn.sigmoid(g))

    @pl.when(c == pl.num_programs(1) - 1)
    def _():
        so_ref[...] = st_ref[...]


def _retention(rq, rk, rv, rg, state0, tabs, ng, nb, n_chunks, chunk):
    m = rq.shape[0]
    decay, read, write, gl = tabs
    row = pl.BlockSpec((chunk, D_RET), lambda b, c: (b * n_chunks + c, 0))
    st = pl.BlockSpec((None, RET_HEADS, RET_DK, RET_DV), lambda b, c: (b, 0, 0, 0))
    cs = lambda shape: pl.BlockSpec(shape, lambda b, c: (0,) * len(shape))
    return pl.pallas_call(
        _retention_kernel,
        grid=(nb, n_chunks),
        in_specs=[row, row, row, row, st, cs(decay.shape), cs(read.shape), cs(write.shape), cs(gl.shape), cs(ng.shape)],
        out_specs=[row, st],
        out_shape=[jax.ShapeDtypeStruct((m, D_RET), F32), jax.ShapeDtypeStruct(state0.shape, F32)],
        scratch_shapes=[pltpu.VMEM((RET_HEADS, RET_DK, RET_DV), F32)],
        compiler_params=_params("arbitrary", "arbitrary"),
        name="retention",
    )(rq, rk, rv, rg, state0, decay, read, write, gl, ng)


def _select_blocks(gate, n_past):
    nb = gate.shape[0]
    blk = lax.broadcasted_iota(jnp.int32, gate.shape, 0)
    valid = blk < n_past
    gm = jnp.where(valid, gate, NEG_INF)
    rank = jnp.zeros(gate.shape, jnp.int32)
    for m in range(nb):
        row = gm[m:m + 1, :]
        ahead = (row > gm) | ((row == gm) & (blk > m))
        rank = rank + ahead.astype(jnp.int32)
    return valid & (rank < MOBA_TOPK) & (jnp.abs(gate) < jnp.inf)


def _moba_prompt_kernel(q_ref, qt_ref, kt_ref, vt_ref, o_ref, km_ref, bias_ref):
    qi = pl.program_id(1)
    nb = kt_ref.shape[-1] // MOBA_BLOCK
    scale = ATT_HD ** -0.5

    @pl.when(qi == 0)
    def _():
        for h in range(ATT_HEADS):
            for n in range(nb):
                kblk = kt_ref[h, :, n * MOBA_BLOCK:(n + 1) * MOBA_BLOCK]
                mean = jnp.sum(kblk, axis=-1, keepdims=True) * (1.0 / MOBA_BLOCK)
                km_ref[h, n] = jnp.broadcast_to(mean, (ATT_HD, Q_BLOCK))

    ob = (qi * Q_BLOCK) // MOBA_BLOCK
    q_off = qi * Q_BLOCK - ob * MOBA_BLOCK
    rows = lax.broadcasted_iota(jnp.int32, (Q_BLOCK, MOBA_BLOCK), 0)
    cols = lax.broadcasted_iota(jnp.int32, (Q_BLOCK, MOBA_BLOCK), 1)
    causal = cols <= rows + q_off

    for h in range(ATT_HEADS):
        qt = qt_ref[h]
        gate = jnp.concatenate(
            [jnp.sum(qt * km_ref[h, n], axis=0, keepdims=True) for n in range(nb)], axis=0)
        sel = _select_blocks(gate, ob)
        bias_t = jnp.where(sel, 0.0, NEG_INF).astype(F32)
        bias_qn = jnp.concatenate([bias_t, jnp.zeros((Q_BLOCK - nb, Q_BLOCK), F32)], axis=0).T
        for n in range(nb):
            bias_ref[n] = jnp.broadcast_to(bias_qn[:, n:n + 1], (Q_BLOCK, Q_BLOCK))

        q = q_ref[:, h * ATT_HD:(h + 1) * ATT_HD].astype(BF16)

        def attend(n, carry, mask_bias):
            m_prev, l_prev, acc = carry
            start = pl.multiple_of(n * MOBA_BLOCK, MOBA_BLOCK)
            kb = kt_ref[h, :, pl.ds(start, MOBA_BLOCK)].astype(BF16)
            vb = vt_ref[h, :, pl.ds(start, MOBA_BLOCK)].astype(BF16)
            s = jnp.dot(q, kb, preferred_element_type=F32) * scale + mask_bias
            m_new = jnp.maximum(m_prev, jnp.max(s, axis=-1, keepdims=True))
            alpha = jnp.exp(m_prev - m_new)
            p = jnp.exp(s - m_new)
            l_new = alpha * l_prev + jnp.sum(p, axis=-1, keepdims=True)
            pv = lax.dot_general(p.astype(BF16), vb, _NT, preferred_element_type=F32)
            return m_new, l_new, alpha * acc + pv

        init = (jnp.full((Q_BLOCK, 1), NEG_INF, F32), jnp.zeros((Q_BLOCK, 1), F32), jnp.zeros((Q_BLOCK, ATT_HD), F32))
        carry = attend(ob, init, jnp.where(causal, 0.0, NEG_INF).astype(F32))

        def past(n, carry):
            b = bias_ref[n]
            return attend(n, carry, jnp.concatenate([b, b], axis=-1))

        m_fin, l_fin, acc = lax.fori_loop(0, ob, past, carry)
        o_ref[:, h * ATT_HD:(h + 1) * ATT_HD] = acc / l_fin


def _moba_prompt(q, qt, kt, vt):
    nbat, _, _, s_len = kt.shape
    nqb = s_len // Q_BLOCK
    nblk = s_len // MOBA_BLOCK
    full = pl.BlockSpec((None, ATT_HEADS, ATT_HD, s_len), lambda b, i: (b, 0, 0, 0))
    return pl.pallas_call(
        _moba_prompt_kernel,
        grid=(nbat, nqb),
        in_specs=[
            pl.BlockSpec((Q_BLOCK, D_ATT), lambda b, i: (b * nqb + i, 0)),
            pl.BlockSpec((None, ATT_HEADS, ATT_HD, Q_BLOCK), lambda b, i: (b, 0, 0, i)),
            full,
            full,
        ],
        out_specs=pl.BlockSpec((Q_BLOCK, D_ATT), lambda b, i: (b * nqb + i, 0)),
        out_shape=jax.ShapeDtypeStruct(q.shape, F32),
        scratch_shapes=[pltpu.VMEM((ATT_HEADS, nblk, ATT_HD, Q_BLOCK), F32), pltpu.VMEM((nblk, Q_BLOCK, Q_BLOCK), F32)],
        compiler_params=_params("arbitrary", "arbitrary"),
        name="moba_prompt",
    )(q, qt, kt, vt)


def _kmean_kernel(pt_ref, *refs):
    del pt_ref
    page_refs, o_ref = refs[:-1], refs[-1]
    j = pl.program_id(2)
    n_per_step = len(page_refs) // PAGES_PER_BLOCK

    @pl.when(j == 0)
    def _():
        o_ref[...] = jnp.zeros(o_ref.shape, F32)

    lane = lax.broadcasted_iota(jnp.int32, o_ref.shape, 1)
    out = o_ref[...]
    for p in range(n_per_step):
        tot = page_refs[PAGES_PER_BLOCK * p][...]
        for r in range(1, PAGES_PER_BLOCK):
            tot = tot + page_refs[PAGES_PER_BLOCK * p + r][...]
        mean = jnp.sum(tot.reshape(D_ATT, PAGE_SIZE), axis=-1, keepdims=True) * (1.0 / MOBA_BLOCK)
        out = jnp.where(lane == j * n_per_step + p, mean, out)
    o_ref[...] = out


def _cache_kmean(cache_t, page_table, nbf, blocks_per_step):
    depth = cache_t.shape[0]
    db = page_table.shape[0]
    pages_per_step = blocks_per_step * PAGES_PER_BLOCK

    def page_spec(i):
        return pl.BlockSpec((None, None, ATT_HEADS, ATT_HD, PAGE_SIZE),
                            lambda l, b, j, pt: (l, pt[b, j * pages_per_step + i], 0, 0, 0))

    return pl.pallas_call(
        _kmean_kernel,
        grid_spec=pltpu.PrefetchScalarGridSpec(
            num_scalar_prefetch=1,
            grid=(depth, db, nbf // blocks_per_step),
            in_specs=[page_spec(i) for i in range(pages_per_step)],
            out_specs=pl.BlockSpec((None, None, D_ATT, nbf), lambda l, b, j, pt: (l, b, 0, 0)),
        ),
        out_shape=jax.ShapeDtypeStruct((depth, db, D_ATT, nbf), F32),
        compiler_params=_params("arbitrary", "arbitrary", "arbitrary"),
        name="cache_kmean",
    )(page_table, *([cache_t] * pages_per_step))


def _topk_kernel(q_ref, km_ref, idx_ref):
    nbf = km_ref.shape[-1]
    n_q = q_ref.shape[0]
    lane = lax.broadcasted_iota(jnp.int32, (n_q, nbf), 1).astype(F32)
    out_lane = lax.broadcasted_iota(jnp.int32, (n_q, 128), 1)
    for h in range(ATT_HEADS):
        q = q_ref[:, h * ATT_HD:(h + 1) * ATT_HD]
        gate = jnp.dot(q, km_ref[h * ATT_HD:(h + 1) * ATT_HD, :], precision=lax.Precision.HIGHEST,
                       preferred_element_type=F32)
        out = jnp.zeros((n_q, 128), jnp.int32)
        for t in range(MOBA_TOPK):
            best = jnp.max(gate, axis=-1, keepdims=True)
            idx = jnp.min(jnp.where(gate == best, lane, float(nbf)), axis=-1, keepdims=True)
            out = jnp.where(out_lane == t, idx.astype(jnp.int32), out)
            gate = jnp.where(lane == idx, NEG_INF, gate)
        idx_ref[h] = out


def _sample_topk(q, kmean_l):
    db, _, nbf = kmean_l.shape
    n_q = q.shape[0] // db
    return pl.pallas_call(
        _topk_kernel,
        grid=(db,),
        in_specs=[pl.BlockSpec((n_q, D_ATT), lambda b: (b, 0)),
                  pl.BlockSpec((None, D_ATT, nbf), lambda b: (b, 0, 0))],
        out_specs=pl.BlockSpec((None, ATT_HEADS, n_q, 128), lambda b: (b, 0, 0, 0)),
        out_shape=jax.ShapeDtypeStruct((db, ATT_HEADS, n_q, 128), jnp.int32),
        compiler_params=_params("arbitrary"),
        name="sample_topk",
    )(q, kmean_l)


def _moba_sample_kernel(pid_ref, qt_ref, ktn_ref, vtn_ref, ck_hbm, cv_hbm, o_ref, kbuf, vbuf, sem, *, layer, n_sel):
    b = pl.program_id(0)
    h = pl.program_id(1)
    n_q = qt_ref.shape[-1]
    n_pages = n_q * n_sel
    base = (b * pl.num_programs(1) + h) * n_pages
    scale = ATT_HD ** -0.5

    def copies(i):
        pid = pid_ref[base + i]
        return (pltpu.make_async_copy(ck_hbm.at[layer, pid, h], kbuf.at[i], sem.at[0]),
                pltpu.make_async_copy(cv_hbm.at[layer, pid, h], vbuf.at[i], sem.at[1]))

    def start(i, _):
        ck, cv = copies(i)
        ck.start()
        cv.start()
        return 0

    def wait(i, _):
        ck, cv = copies(i)
        ck.wait()
        cv.wait()
        return 0

    lax.fori_loop(0, n_pages, start, 0)
    lax.fori_loop(0, n_pages, wait, 0)

    qt = qt_ref[...]
    ktn = ktn_ref[...]
    vtn = vtn_ref[...]
    pos = lax.broadcasted_iota(jnp.int32, (1, n_q), 1)
    for l in range(n_q):
        qc = qt[:, l:l + 1]
        s_new = jnp.sum(qc * ktn, axis=0, keepdims=True) * scale
        s_new = jnp.where(pos <= l, s_new, NEG_INF)
        qb = jnp.broadcast_to(qc, (ATT_HD, PAGE_SIZE))
        s_sel = [jnp.sum(qb * kbuf[l * n_sel + j], axis=0, keepdims=True) * scale for j in range(n_sel)]
        m = jnp.max(s_new, axis=-1, keepdims=True)
        for s in s_sel:
            m = jnp.maximum(m, jnp.max(s, axis=-1, keepdims=True))
        p_new = jnp.exp(s_new - m)
        denom = jnp.sum(p_new, axis=-1, keepdims=True)
        acc = jnp.zeros((ATT_HD, PAGE_SIZE), F32)
        for j, s in enumerate(s_sel):
            p = jnp.exp(s - m)
            denom = denom + jnp.sum(p, axis=-1, keepdims=True)
            acc = acc + p * vbuf[l * n_sel + j]
        out = jnp.sum(acc, axis=-1, keepdims=True) + jnp.sum(p_new * vtn, axis=-1, keepdims=True)
        o_ref[:, l:l + 1] = out / denom


def _moba_sample(pids, qt, ktn, vtn, cache_k_t, cache_v_t, layer):
    db, _, _, n_q = qt.shape
    n_sel = pids.shape[0] // (db * ATT_HEADS * n_q)
    small = pl.BlockSpec((None, None, ATT_HD, n_q), lambda b, h, pid: (b, h, 0, 0))
    hbm = pl.BlockSpec(memory_space=pl.ANY)
    return pl.pallas_call(
        functools.partial(_moba_sample_kernel, layer=layer, n_sel=n_sel),
        grid_spec=pltpu.PrefetchScalarGridSpec(
            num_scalar_prefetch=1,
            grid=(db, ATT_HEADS),
            in_specs=[small, small, small, hbm, hbm],
            out_specs=small,
            scratch_shapes=[pltpu.VMEM((n_q * n_sel, ATT_HD, PAGE_SIZE), F32),
                            pltpu.VMEM((n_q * n_sel, ATT_HD, PAGE_SIZE), F32),
                            pltpu.SemaphoreType.DMA((2,))],
        ),
        out_shape=jax.ShapeDtypeStruct(qt.shape, F32),
        compiler_params=_params("arbitrary", "arbitrary"),
        name="moba_sample",
    )(pids, qt, ktn, vtn, cache_k_t, cache_v_t)


def _mix_ffn_kernel(yr_ref, ao_ref, h_ref, wo_ref, g_ref, wg_ref, wu_ref, wd_ref, o_ref, *, ff_chunk):
    h1 = (h_ref[...]
          + jnp.dot(yr_ref[...].astype(BF16), wo_ref[0:D_RET, :], preferred_element_type=F32)
          + jnp.dot(ao_ref[...].astype(BF16), wo_ref[D_RET:D_RET + D_ATT, :], preferred_element_type=F32))
    hn = (h1 * lax.rsqrt(jnp.mean(h1 * h1, axis=-1, keepdims=True) + RMS_EPS) * g_ref[...]).astype(BF16)
    acc = jnp.zeros_like(h1)
    d_ff = wg_ref.shape[1]
    for c in range(d_ff // ff_chunk):
        sl = slice(c * ff_chunk, (c + 1) * ff_chunk)
        gate = jnp.dot(hn, wg_ref[:, sl], preferred_element_type=F32)
        up = jnp.dot(hn, wu_ref[:, sl], preferred_element_type=F32)
        act = (gate * jax.nn.sigmoid(gate) * up).astype(BF16)
        acc = acc + jnp.dot(act, wd_ref[sl, :], preferred_element_type=F32)
    o_ref[...] = h1 + acc


def _mix_ffn(yr, ao, h, wo, g, wg, wu, wd, tm):
    m, d = h.shape
    row = lambda w: pl.BlockSpec((tm, w), lambda i: (i, 0))
    return pl.pallas_call(
        functools.partial(_mix_ffn_kernel, ff_chunk=256),
        grid=(m // tm,),
        in_specs=[row(D_RET), row(D_ATT), row(d), _const_spec(wo.shape), _const_spec((1, d)),
                  _const_spec(wg.shape), _const_spec(wu.shape), _const_spec(wd.shape)],
        out_specs=row(d),
        out_shape=jax.ShapeDtypeStruct((m, d), F32),
        compiler_params=_params("arbitrary"),
        name="mix_ffn",
    )(yr, ao, h, wo, g, wg, wu, wd)


def _rope_tables(pos):
    ret_freq = 1.0 / (ROPE_THETA ** jnp.linspace(0.0, 1.0, RET_DK // 2, dtype=F32))
    att_freq = 1.0 / (ROPE_THETA ** (jnp.arange(0, ATT_HD, 2, dtype=F32) / ATT_HD))
    ang_r = pos.astype(F32)[:, None] * ret_freq[None, :]
    ang_a = pos.astype(F32)[:, None] * att_freq[None, :]
    rcos = jnp.concatenate([jnp.cos(ang_r), jnp.cos(ang_r)], axis=-1)
    rsin = jnp.concatenate([-jnp.sin(ang_r), jnp.sin(ang_r)], axis=-1)
    return rcos, rsin, jnp.cos(ang_a).T, jnp.sin(ang_a).T


def _retention_tables(chunk):
    log_gamma = jnp.log1p(-jnp.exp2(-5.0 - jnp.arange(RET_HEADS, dtype=F32)))
    i = jnp.arange(chunk, dtype=F32)
    diff = i[:, None] - i[None, :]
    decay = jnp.where(diff >= 0, jnp.exp(log_gamma[:, None, None] * jnp.maximum(diff, 0.0)), 0.0)
    read = jnp.exp(log_gamma[None, :] * (i[:, None] + 1.0))
    write = jnp.exp(log_gamma[None, :] * (chunk - 1.0 - i[:, None]))
    gl = jnp.exp(log_gamma * chunk)[None, :]
    wide = lambda t: jnp.repeat(t, RET_DV, axis=1)
    return decay, wide(read), wide(write), wide(gl)


def kernel(x_prompt, x_sample, cache_k, cache_v, state_ret, page_table, norm1_g, w_in, q_norm_g, k_norm_g,
           ret_norm_g, w_out, norm2_g, w_gate, w_up, w_down):
    nbat, s_len, d_model = x_prompt.shape
    db, n_q, _ = x_sample.shape
    depth = w_in.shape[0]
    n_pages = page_table.shape[1]
    past = n_pages * PAGE_SIZE
    nbf = past // MOBA_BLOCK
    assert n_pages % PAGES_PER_BLOCK == 0, "the new tokens must start a fresh MoBA block"
    assert nbf >= MOBA_TOPK and s_len % MOBA_BLOCK == 0 and s_len % RET_CHUNK == 0
    n_sel = MOBA_TOPK * PAGES_PER_BLOCK

    cache_k_t = jnp.transpose(cache_k, (0, 1, 3, 4, 2))
    cache_v_t = jnp.transpose(cache_v, (0, 1, 3, 4, 2))

    tabs_p = _rope_tables(jnp.arange(s_len))
    rc, rs, ac, as_ = _rope_tables(past + jnp.arange(n_q))
    tabs_s = (jnp.tile(rc, (db, 1)), jnp.tile(rs, (db, 1)), jnp.tile(ac, (1, db)), jnp.tile(as_, (1, db)))
    rtab_p = _retention_tables(RET_CHUNK)
    rtab_s = _retention_tables(n_q)

    blocks_per_step = 8 if nbf % 8 == 0 else 1
    kmean = _cache_kmean(cache_k_t, page_table, nbf, blocks_per_step)

    hp = x_prompt.reshape(nbat * s_len, d_model)
    hs = x_sample.reshape(db * n_q, d_model)
    zero_state = jnp.zeros((nbat, RET_HEADS, RET_DK, RET_DV), F32)
    tm_p = 512 if s_len % 512 == 0 else RET_CHUNK
    kp, vp, rp, kss, vss, rss = [], [], [], [], [], []
    d_split = 2 * RET_HEADS * RET_DK + 2 * D_RET
    for l in range(depth):
        ws = w_in[l, :, :d_split].astype(BF16)
        wt = w_in[l, :, d_split:].T.astype(BF16)
        g1 = norm1_g[l][None, :]
        g2 = norm2_g[l][None, :]
        qg = q_norm_g[l][:, None]
        kg = k_norm_g[l][:, None]
        ng = ret_norm_g[l].reshape(1, D_RET)
        wo = w_out[l].astype(BF16)
        wg = w_gate[l].astype(BF16)
        wu = w_up[l].astype(BF16)
        wd = w_down[l].astype(BF16)

        rq, rk, rv, rg, aq, aqt, akt, avt = _inproj(hp, g1, ws, wt, tabs_p, qg, kg, nbat, s_len, tm_p)
        yr, rst = _retention(rq, rk, rv, rg, zero_state, rtab_p, ng, nbat, s_len // RET_CHUNK, RET_CHUNK)
        ao = _moba_prompt(aq, aqt, akt, avt)
        hp = _mix_ffn(yr, ao, hp, wo, g2, wg, wu, wd, tm_p)
        kp.append(akt)
        vp.append(avt)
        rp.append(rst)

        m_s = db * n_q
        rq, rk, rv, rg, aq, aqt, akt, avt = _inproj(hs, g1, ws, wt, tabs_s, qg, kg, 1, m_s, m_s)
        yr, rst = _retention(rq, rk, rv, rg, state_ret[l], rtab_s, ng, db, 1, n_q)
        gidx = _sample_topk(aq, kmean[l])[..., :MOBA_TOPK]
        slot = gidx[..., None] * PAGES_PER_BLOCK + jnp.arange(PAGES_PER_BLOCK)
        pids = page_table[jnp.arange(db)[:, None, None, None, None], slot].reshape(-1)
        per_seq = lambda t: t.reshape(ATT_HEADS, ATT_HD, db, n_q).transpose(2, 0, 1, 3)
        aot = _moba_sample(pids, per_seq(aqt), per_seq(akt), per_seq(avt), cache_k_t, cache_v_t, l)
        ao = aot.transpose(0, 3, 1, 2).reshape(m_s, D_ATT)
        hs = _mix_ffn(yr, ao, hs, wo, g2, wg, wu, wd, m_s)
        rows = lambda t: t.reshape(D_ATT, db, n_q).transpose(1, 2, 0).reshape(db, n_q, ATT_HEADS, ATT_HD)
        kss.append(rows(akt))
        vss.append(rows(avt))
        rss.append(rst)

    seq_major = lambda ts: jnp.transpose(jnp.stack(ts), (0, 1, 4, 2, 3))
    return (hp.reshape(nbat, s_len, d_model), hs.reshape(db, n_q, d_model), seq_major(kp), seq_major(vp),
            jnp.stack(rp), jnp.stack(kss), jnp.stack(vss), jnp.stack(rss))
```

```python
import functools

import jax
import jax.numpy as jnp
from jax import lax
from jax.experimental import pallas as pl
from jax.experimental.pallas import tpu as pltpu

F32 = jnp.float32
BF16 = jnp.bfloat16

PAGE_SIZE = 128
RET_HEADS = 4
RET_DK = 128
RET_DV = 128
RET_CHUNK = 128
ATT_HEADS = 8
ATT_HD = 64
MOBA_BLOCK = 256
MOBA_TOPK = 3
Q_BLOCK = 128
ROPE_THETA = 10000.0
RMS_EPS = 1e-6
D_RET = RET_HEADS * RET_DV
D_ATT = ATT_HEADS * ATT_HD
PAGES_PER_BLOCK = MOBA_BLOCK // PAGE_SIZE

VMEM_LIMIT_BYTES = 56 * 1024 * 1024
NEG_INF = float("-inf")

_NT = (((1,), (1,)), ((), ()))
_TN = (((0,), (0,)), ((), ()))


def _params(*sem):
    return pltpu.CompilerParams(dimension_semantics=sem, vmem_limit_bytes=VMEM_LIMIT_BYTES)


def _const_spec(shape):
    nd = len(shape)
    return pl.BlockSpec(shape, lambda *_: (0,) * nd, pipeline_mode=pl.Buffered(1))


def _inproj_kernel(x_ref, g_ref, ws_ref, wt_ref, rcos_ref, rsin_ref, acos_ref, asin_ref, qg_ref, kg_ref,
                   rq_ref, rk_ref, rv_ref, rg_ref, q_ref, qt_ref, kt_ref, vt_ref, kb_ref, vtb_ref):
    x = x_ref[...]
    xn = (x * lax.rsqrt(jnp.mean(x * x, axis=-1, keepdims=True) + RMS_EPS) * g_ref[...]).astype(BF16)
    tm = x.shape[0]

    def std(c):
        return jnp.dot(xn, ws_ref[:, c * D_RET:(c + 1) * D_RET], preferred_element_type=F32)

    rcos = rcos_ref[...]
    rsin = rsin_ref[...]
    for c, o_ref, scale in ((0, rq_ref, None), (1, rk_ref, RET_DK ** -0.5)):
        acc = std(c)
        for h in range(RET_HEADS):
            xh = acc[:, h * RET_DK:(h + 1) * RET_DK]
            r = xh * rcos + pltpu.roll(xh, RET_DK // 2, 1) * rsin
            o_ref[:, h * RET_DK:(h + 1) * RET_DK] = r if scale is None else r * scale
    rv_ref[...] = std(2)
    rg_ref[...] = std(3)

    def tr(c):
        return lax.dot_general(wt_ref[c * D_ATT:(c + 1) * D_ATT, :], xn, _NT, preferred_element_type=F32)

    acos = acos_ref[...]
    asin = asin_ref[...]
    half = ATT_HD // 2
    for c, gn_ref, o_ref in ((0, qg_ref, qt_ref), (1, kg_ref, kt_ref)):
        acc = tr(c)
        for h in range(ATT_HEADS):
            xh = acc[h * ATT_HD:(h + 1) * ATT_HD, :]
            y = xh * lax.rsqrt(jnp.mean(xh * xh, axis=0, keepdims=True) + RMS_EPS) * gn_ref[...]
            y1 = y[:half]
            y2 = y[half:]
            o_ref[h, 0:half, :] = y1 * acos - y2 * asin
            o_ref[h, half:ATT_HD, :] = y2 * acos + y1 * asin
    q_ref[...] = qt_ref[...].reshape(D_ATT, tm).T
    kb_ref[...] = kt_ref[...].reshape(D_ATT, tm).T.astype(BF16)
    vt = tr(2).reshape(ATT_HEADS, ATT_HD, tm)
    vt_ref[...] = vt
    vtb_ref[...] = vt.astype(BF16)


def _inproj(x, g, ws, wt, tabs, qg, kg, nb, s_len, tm):
    m, d = x.shape
    nt = s_len // tm
    rcos, rsin, acos_t, asin_t = tabs
    row = lambda i: (i, 0)
    row_out = pl.BlockSpec((tm, D_RET), row)
    t_out = pl.BlockSpec((None, ATT_HEADS, ATT_HD, tm), lambda i: (i // nt, 0, 0, i % nt))
    t_shape = jax.ShapeDtypeStruct((nb, ATT_HEADS, ATT_HD, s_len), F32)
    r_shape = jax.ShapeDtypeStruct((m, D_RET), F32)
    return pl.pallas_call(
        _inproj_kernel,
        grid=(m // tm,),
        in_specs=[
            pl.BlockSpec((tm, d), row),
            _const_spec((1, d)),
            _const_spec(ws.shape),
            _const_spec(wt.shape),
            pl.BlockSpec((tm, RET_DK), lambda i: (i % nt, 0)),
            pl.BlockSpec((tm, RET_DK), lambda i: (i % nt, 0)),
            pl.BlockSpec((ATT_HD // 2, tm), lambda i: (0, i % nt)),
            pl.BlockSpec((ATT_HD // 2, tm), lambda i: (0, i % nt)),
            _const_spec((ATT_HD, 1)),
            _const_spec((ATT_HD, 1)),
        ],
        out_specs=[row_out, row_out, row_out, row_out, row_out, t_out, t_out, t_out, row_out, t_out],
        out_shape=[r_shape, r_shape, r_shape, r_shape, r_shape, t_shape, t_shape, t_shape,
                   jax.ShapeDtypeStruct((m, D_ATT), BF16), jax.ShapeDtypeStruct(t_shape.shape, BF16)],
        compiler_params=_params("arbitrary"),
        name="inproj",
    )(x, g, ws, wt, rcos, rsin, acos_t, asin_t, qg, kg)


def _retention_kernel(rq_ref, rk_ref, rv_ref, rg_ref, s0_ref, decay_ref, read_ref, write_ref, gl_ref, ng_ref,
                      y_ref, so_ref, st_ref):
    c = pl.program_id(1)

    @pl.when(c == 0)
    def _():
        st_ref[...] = s0_ref[...]

    for h in range(RET_HEADS):
        sl = slice(h * RET_DK, (h + 1) * RET_DK)
        q = rq_ref[:, sl].astype(BF16)
        k = rk_ref[:, sl]
        v = rv_ref[:, sl].astype(BF16)
        state = st_ref[h]
        s = lax.dot_general(q, k.astype(BF16), _NT, preferred_element_type=F32) * decay_ref[h]
        o = (jnp.dot(s.astype(BF16), v, preferred_element_type=F32)
             + jnp.dot(q, state.astype(BF16), preferred_element_type=F32) * read_ref[:, sl])
        kw = (k * write_ref[:, sl]).astype(BF16)
        st_ref[h] = gl_ref[:, sl] * state + lax.dot_general(kw, v, _TN, preferred_element_type=F32)
        y = o * lax.rsqrt(jnp.mean(o * o, axis=-1, keepdims=True) + RMS_EPS) * ng_ref[:, sl]
        g = rg_ref[:, sl]
        y_ref[:, sl] = y * (g * jax.nn.sigmoid(g))

    @pl.when(c == pl.num_programs(1) - 1)
    def _():
        so_ref[...] = st_ref[...]


def _retention(rq, rk, rv, rg, state0, tabs, ng, nb, n_chunks, chunk):
    m = rq.shape[0]
    decay, read, write, gl = tabs
    row = pl.BlockSpec((chunk, D_RET), lambda b, c: (b * n_chunks + c, 0))
    st = pl.BlockSpec((None, RET_HEADS, RET_DK, RET_DV), lambda b, c: (b, 0, 0, 0))
    cs = lambda shape: pl.BlockSpec(shape, lambda b, c: (0,) * len(shape))
    return pl.pallas_call(
        _retention_kernel,
        grid=(nb, n_chunks),
        in_specs=[row, row, row, row, st, cs(decay.shape), cs(read.shape), cs(write.shape), cs(gl.shape), cs(ng.shape)],
        out_specs=[row, st],
        out_shape=[jax.ShapeDtypeStruct((m, D_RET), F32), jax.ShapeDtypeStruct(state0.shape, F32)],
        scratch_shapes=[pltpu.VMEM((RET_HEADS, RET_DK, RET_DV), F32)],
        compiler_params=_params("arbitrary", "arbitrary"),
        name="retention",
    )(rq, rk, rv, rg, state0, decay, read, write, gl, ng)


def _select_blocks(gate, n_past):
    nb = gate.shape[0]
    blk = lax.broadcasted_iota(jnp.int32, gate.shape, 0)
    valid = blk < n_past
    gm = jnp.where(valid, gate, NEG_INF)
    rank = jnp.zeros(gate.shape, jnp.int32)
    for m in range(nb):
        row = gm[m:m + 1, :]
        ahead = (row > gm) | ((row == gm) & (blk > m))
        rank = rank + ahead.astype(jnp.int32)
    return valid & (rank < MOBA_TOPK) & (jnp.abs(gate) < jnp.inf)


def _moba_prompt_kernel(qt_ref, kt_ref, kb_ref, vt_ref, o_ref, km_ref, bias_ref, m_ref, l_ref, acc_ref):
    qi = pl.program_id(1)
    nb = kt_ref.shape[-1] // MOBA_BLOCK
    n_pairs = ATT_HEADS // 2

    @pl.when(qi == 0)
    def _():
        for h in range(ATT_HEADS):
            for n in range(nb):
                kblk = kt_ref[h, :, n * MOBA_BLOCK:(n + 1) * MOBA_BLOCK]
                mean = jnp.sum(kblk, axis=-1, keepdims=True) * (1.0 / MOBA_BLOCK)
                km_ref[h, n] = jnp.broadcast_to(mean, (ATT_HD, Q_BLOCK))

    ob = (qi * Q_BLOCK) // MOBA_BLOCK
    q_off = qi * Q_BLOCK - ob * MOBA_BLOCK

    q_pairs = []
    zero = jnp.zeros((ATT_HD, Q_BLOCK), BF16)
    for h in range(ATT_HEADS):
        qt = qt_ref[h]
        gate = jnp.concatenate(
            [jnp.sum(qt * km_ref[h, n], axis=0, keepdims=True) for n in range(nb)], axis=0)
        bias_ref[h] = jnp.where(_select_blocks(gate, ob), 0.0, NEG_INF).astype(F32)
        qs = (qt * ATT_HD ** -0.5).astype(BF16)
        if h % 2 == 0:
            top = jnp.concatenate([qs, zero], axis=1)
        else:
            q_pairs.append(jnp.concatenate([top, jnp.concatenate([zero, qs], axis=1)], axis=0))

    def scores(n, pair):
        start = pl.multiple_of(n * MOBA_BLOCK, MOBA_BLOCK)
        kb = kb_ref[pl.ds(start, MOBA_BLOCK), pair * 2 * ATT_HD:(pair + 1) * 2 * ATT_HD]
        return jnp.dot(kb, q_pairs[pair], preferred_element_type=F32)

    def values(n, h):
        start = pl.multiple_of(n * MOBA_BLOCK, MOBA_BLOCK)
        return vt_ref[h, :, pl.ds(start, MOBA_BLOCK)]

    key = lax.broadcasted_iota(jnp.int32, (MOBA_BLOCK, Q_BLOCK), 0)
    qry = lax.broadcasted_iota(jnp.int32, (MOBA_BLOCK, Q_BLOCK), 1)
    causal = jnp.where(key <= qry + q_off, 0.0, NEG_INF).astype(F32)
    for pair in range(n_pairs):
        st = scores(ob, pair)
        for j in range(2):
            h = 2 * pair + j
            s = st[:, j * Q_BLOCK:(j + 1) * Q_BLOCK] + causal
            m = jnp.max(s, axis=0, keepdims=True)
            p = jnp.exp(s - m)
            m_ref[h] = m
            l_ref[h] = jnp.sum(p, axis=0, keepdims=True)
            acc_ref[h] = jnp.dot(values(ob, h), p.astype(BF16), preferred_element_type=F32)

    def past(n, _):
        for pair in range(n_pairs):
            st = scores(n, pair)
            for j in range(2):
                h = 2 * pair + j
                s = st[:, j * Q_BLOCK:(j + 1) * Q_BLOCK] + bias_ref[h, pl.ds(n, 1), :]
                m_prev = m_ref[h]
                m_new = jnp.maximum(m_prev, jnp.max(s, axis=0, keepdims=True))
                alpha = jnp.exp(m_prev - m_new)
                p = jnp.exp(s - m_new)
                m_ref[h] = m_new
                l_ref[h] = alpha * l_ref[h] + jnp.sum(p, axis=0, keepdims=True)
                acc_ref[h] = alpha * acc_ref[h] + jnp.dot(values(n, h), p.astype(BF16), preferred_element_type=F32)
        return 0

    lax.fori_loop(0, ob, past, 0)
    out_t = jnp.concatenate([acc_ref[h] / l_ref[h] for h in range(ATT_HEADS)], axis=0)
    o_ref[...] = out_t.T


def _moba_prompt(qt, kt, kb, vtb, m_rows):
    nbat, _, _, s_len = kt.shape
    nqb = s_len // Q_BLOCK
    nblk = s_len // MOBA_BLOCK
    full = pl.BlockSpec((None, ATT_HEADS, ATT_HD, s_len), lambda b, i: (b, 0, 0, 0))
    return pl.pallas_call(
        _moba_prompt_kernel,
        grid=(nbat, nqb),
        in_specs=[
            pl.BlockSpec((None, ATT_HEADS, ATT_HD, Q_BLOCK), lambda b, i: (b, 0, 0, i)),
            full,
            pl.BlockSpec((s_len, D_ATT), lambda b, i: (b, 0)),
            full,
        ],
        out_specs=pl.BlockSpec((Q_BLOCK, D_ATT), lambda b, i: (b * nqb + i, 0)),
        out_shape=jax.ShapeDtypeStruct((m_rows, D_ATT), F32),
        scratch_shapes=[pltpu.VMEM((ATT_HEADS, nblk, ATT_HD, Q_BLOCK), F32),
                        pltpu.VMEM((ATT_HEADS, nblk, Q_BLOCK), F32),
                        pltpu.VMEM((ATT_HEADS, 1, Q_BLOCK), F32),
                        pltpu.VMEM((ATT_HEADS, 1, Q_BLOCK), F32),
                        pltpu.VMEM((ATT_HEADS, ATT_HD, Q_BLOCK), F32)],
        compiler_params=_params("arbitrary", "arbitrary"),
        name="moba_prompt",
    )(qt, kt, kb, vtb)


def _kmean_kernel(pt_ref, *refs):
    del pt_ref
    page_refs, o_ref = refs[:-1], refs[-1]
    j = pl.program_id(2)
    n_per_step = len(page_refs) // PAGES_PER_BLOCK

    @pl.when(j == 0)
    def _():
        o_ref[...] = jnp.zeros(o_ref.shape, F32)

    lane = lax.broadcasted_iota(jnp.int32, o_ref.shape, 1)
    out = o_ref[...]
    for p in range(n_per_step):
        tot = page_refs[PAGES_PER_BLOCK * p][...]
        for r in range(1, PAGES_PER_BLOCK):
            tot = tot + page_refs[PAGES_PER_BLOCK * p + r][...]
        mean = jnp.sum(tot.reshape(D_ATT, PAGE_SIZE), axis=-1, keepdims=True) * (1.0 / MOBA_BLOCK)
        out = jnp.where(lane == j * n_per_step + p, mean, out)
    o_ref[...] = out


def _cache_kmean(cache_t, page_table, nbf, blocks_per_step):
    depth = cache_t.shape[0]
    db = page_table.shape[0]
    pages_per_step = blocks_per_step * PAGES_PER_BLOCK

    def page_spec(i):
        return pl.BlockSpec((None, None, ATT_HEADS, ATT_HD, PAGE_SIZE),
                            lambda l, b, j, pt: (l, pt[b, j * pages_per_step + i], 0, 0, 0))

    return pl.pallas_call(
        _kmean_kernel,
        grid_spec=pltpu.PrefetchScalarGridSpec(
            num_scalar_prefetch=1,
            grid=(depth, db, nbf // blocks_per_step),
            in_specs=[page_spec(i) for i in range(pages_per_step)],
            out_specs=pl.BlockSpec((None, None, D_ATT, nbf), lambda l, b, j, pt: (l, b, 0, 0)),
        ),
        out_shape=jax.ShapeDtypeStruct((depth, db, D_ATT, nbf), F32),
        compiler_params=_params("arbitrary", "arbitrary", "arbitrary"),
        name="cache_kmean",
    )(page_table, *([cache_t] * pages_per_step))


def _topk_kernel(q_ref, km_ref, idx_ref):
    nbf = km_ref.shape[-1]
    n_q = q_ref.shape[0]
    lane = lax.broadcasted_iota(jnp.int32, (n_q, nbf), 1).astype(F32)
    out_lane = lax.broadcasted_iota(jnp.int32, (n_q, 128), 1)
    for h in range(ATT_HEADS):
        q = q_ref[:, h * ATT_HD:(h + 1) * ATT_HD]
        gate = jnp.dot(q, km_ref[h * ATT_HD:(h + 1) * ATT_HD, :], precision=lax.Precision.HIGHEST,
                       preferred_element_type=F32)
        out = jnp.zeros((n_q, 128), jnp.int32)
        for t in range(MOBA_TOPK):
            best = jnp.max(gate, axis=-1, keepdims=True)
            idx = jnp.min(jnp.where(gate == best, lane, float(nbf)), axis=-1, keepdims=True)
            out = jnp.where(out_lane == t, idx.astype(jnp.int32), out)
            gate = jnp.where(lane == idx, NEG_INF, gate)
        idx_ref[h] = out


def _sample_topk(q, kmean_l):
    db, _, nbf = kmean_l.shape
    n_q = q.shape[0] // db
    return pl.pallas_call(
        _topk_kernel,
        grid=(db,),
        in_specs=[pl.BlockSpec((n_q, D_ATT), lambda b: (b, 0)),
                  pl.BlockSpec((None, D_ATT, nbf), lambda b: (b, 0, 0))],
        out_specs=pl.BlockSpec((None, ATT_HEADS, n_q, 128), lambda b: (b, 0, 0, 0)),
        out_shape=jax.ShapeDtypeStruct((db, ATT_HEADS, n_q, 128), jnp.int32),
        compiler_params=_params("arbitrary"),
        name="sample_topk",
    )(q, kmean_l)


def _moba_sample_kernel(gidx_ref, pt_ref, qt_ref, ktn_ref, vtn_ref, ck_hbm, cv_hbm, o_ref, kbuf, vbuf, sem, *, layer):
    n_heads = pl.num_programs(1)
    step = pl.program_id(0) * n_heads + pl.program_id(1)
    n_steps = pl.num_programs(0) * n_heads
    n_q = qt_ref.shape[-1]
    n_sel = MOBA_TOPK * PAGES_PER_BLOCK
    scale = ATT_HD ** -0.5

    def for_each_copy(step_, slot_, fn):
        b_ = step_ // n_heads
        h_ = step_ % n_heads
        base = step_ * (n_q * MOBA_TOPK)

        def body(g, _):
            blk = gidx_ref[base + g]
            for r in range(PAGES_PER_BLOCK):
                pid = pt_ref[b_, blk * PAGES_PER_BLOCK + r]
                i = g * PAGES_PER_BLOCK + r
                fn(pltpu.make_async_copy(ck_hbm.at[layer, pid, h_], kbuf.at[slot_, i], sem.at[slot_, 0]))
                fn(pltpu.make_async_copy(cv_hbm.at[layer, pid, h_], vbuf.at[slot_, i], sem.at[slot_, 1]))
            return 0

        lax.fori_loop(0, n_q * MOBA_TOPK, body, 0, unroll=4)

    def start_all(step_, slot_):
        for_each_copy(step_, slot_, lambda c: c.start())

    def wait_all(step_, slot_):
        for_each_copy(step_, slot_, lambda c: c.wait())

    slot = step % 2

    @pl.when(step == 0)
    def _():
        start_all(step, slot)

    @pl.when(step + 1 < n_steps)
    def _():
        start_all(step + 1, 1 - slot)

    wait_all(step, slot)

    qt = qt_ref[...]
    ktn = ktn_ref[...]
    vtn = vtn_ref[...]
    q_cols = [qt[:, l:l + 1] for l in range(n_q)]
    q_wide = [jnp.broadcast_to(qc, (ATT_HD, PAGE_SIZE)) for qc in q_cols]

    def rows(fn):
        return jnp.concatenate([fn(l) for l in range(n_q)], axis=0)

    s_new = rows(lambda l: jnp.sum(q_cols[l] * ktn, axis=0, keepdims=True)) * scale
    qry = lax.broadcasted_iota(jnp.int32, (n_q, n_q), 0)
    key = lax.broadcasted_iota(jnp.int32, (n_q, n_q), 1)
    s_new = jnp.where(key <= qry, s_new, NEG_INF)
    s_sel = [rows(lambda l: jnp.sum(q_wide[l] * kbuf[slot, l * n_sel + j], axis=0, keepdims=True)) * scale
             for j in range(n_sel)]
    m = jnp.max(s_new, axis=-1, keepdims=True)
    for s in s_sel:
        m = jnp.maximum(m, jnp.max(s, axis=-1, keepdims=True))
    p_new = jnp.exp(s_new - m)
    p_sel = [jnp.exp(s - m) for s in s_sel]
    denom = jnp.sum(p_new, axis=-1, keepdims=True)
    for p in p_sel:
        denom = denom + jnp.sum(p, axis=-1, keepdims=True)
    for l in range(n_q):
        acc = p_sel[0][l:l + 1, :] * vbuf[slot, l * n_sel]
        for j in range(1, n_sel):
            acc = acc + p_sel[j][l:l + 1, :] * vbuf[slot, l * n_sel + j]
        out = jnp.sum(acc, axis=-1, keepdims=True) + jnp.sum(p_new[l:l + 1, :] * vtn, axis=-1, keepdims=True)
        o_ref[:, l:l + 1] = out / denom[l:l + 1, :]


def _moba_sample(gidx, page_table, qt, ktn, vtn, cache_k_t, cache_v_t, layer):
    db, _, _, n_q = qt.shape
    n_pages = n_q * MOBA_TOPK * PAGES_PER_BLOCK
    small = pl.BlockSpec((None, None, ATT_HD, n_q), lambda b, h, gi, pt: (b, h, 0, 0))
    hbm = pl.BlockSpec(memory_space=pl.ANY)
    return pl.pallas_call(
        functools.partial(_moba_sample_kernel, layer=layer),
        grid_spec=pltpu.PrefetchScalarGridSpec(
            num_scalar_prefetch=2,
            grid=(db, ATT_HEADS),
            in_specs=[small, small, small, hbm, hbm],
            out_specs=small,
            scratch_shapes=[pltpu.VMEM((2, n_pages, ATT_HD, PAGE_SIZE), F32),
                            pltpu.VMEM((2, n_pages, ATT_HD, PAGE_SIZE), F32),
                            pltpu.SemaphoreType.DMA((2, 2))],
        ),
        out_shape=jax.ShapeDtypeStruct(qt.shape, F32),
        compiler_params=_params("arbitrary", "arbitrary"),
        name="moba_sample",
    )(gidx, page_table, qt, ktn, vtn, cache_k_t, cache_v_t)


def _mix_ffn_kernel(yr_ref, ao_ref, h_ref, wo_ref, g_ref, wg_ref, wu_ref, wd_ref, o_ref, *, ff_chunk):
    h1 = (h_ref[...]
          + jnp.dot(yr_ref[...].astype(BF16), wo_ref[0:D_RET, :], preferred_element_type=F32)
          + jnp.dot(ao_ref[...].astype(BF16), wo_ref[D_RET:D_RET + D_ATT, :], preferred_element_type=F32))
    hn = (h1 * lax.rsqrt(jnp.mean(h1 * h1, axis=-1, keepdims=True) + RMS_EPS) * g_ref[...]).astype(BF16)
    acc = jnp.zeros_like(h1)
    d_ff = wg_ref.shape[1]
    for c in range(d_ff // ff_chunk):
        sl = slice(c * ff_chunk, (c + 1) * ff_chunk)
        gate = jnp.dot(hn, wg_ref[:, sl], preferred_element_type=F32)
        up = jnp.dot(hn, wu_ref[:, sl], preferred_element_type=F32)
        act = (gate * jax.nn.sigmoid(gate) * up).astype(BF16)
        acc = acc + jnp.dot(act, wd_ref[sl, :], preferred_element_type=F32)
    o_ref[...] = h1 + acc


def _mix_ffn(yr, ao, h, wo, g, wg, wu, wd, tm):
    m, d = h.shape
    row = lambda w: pl.BlockSpec((tm, w), lambda i: (i, 0))
    return pl.pallas_call(
        functools.partial(_mix_ffn_kernel, ff_chunk=256),
        grid=(m // tm,),
        in_specs=[row(D_RET), row(D_ATT), row(d), _const_spec(wo.shape), _const_spec((1, d)),
                  _const_spec(wg.shape), _const_spec(wu.shape), _const_spec(wd.shape)],
        out_specs=row(d),
        out_shape=jax.ShapeDtypeStruct((m, d), F32),
        compiler_params=_params("arbitrary"),
        name="mix_ffn",
    )(yr, ao, h, wo, g, wg, wu, wd)


def _rope_tables(pos):
    ret_freq = 1.0 / (ROPE_THETA ** jnp.linspace(0.0, 1.0, RET_DK // 2, dtype=F32))
    att_freq = 1.0 / (ROPE_THETA ** (jnp.arange(0, ATT_HD, 2, dtype=F32) / ATT_HD))
    ang_r = pos.astype(F32)[:, None] * ret_freq[None, :]
    ang_a = pos.astype(F32)[:, None] * att_freq[None, :]
    rcos = jnp.concatenate([jnp.cos(ang_r), jnp.cos(ang_r)], axis=-1)
    rsin = jnp.concatenate([-jnp.sin(ang_r), jnp.sin(ang_r)], axis=-1)
    return rcos, rsin, jnp.cos(ang_a).T, jnp.sin(ang_a).T


def _retention_tables(chunk):
    log_gamma = jnp.log1p(-jnp.exp2(-5.0 - jnp.arange(RET_HEADS, dtype=F32)))
    i = jnp.arange(chunk, dtype=F32)
    diff = i[:, None] - i[None, :]
    decay = jnp.where(diff >= 0, jnp.exp(log_gamma[:, None, None] * jnp.maximum(diff, 0.0)), 0.0)
    read = jnp.exp(log_gamma[None, :] * (i[:, None] + 1.0))
    write = jnp.exp(log_gamma[None, :] * (chunk - 1.0 - i[:, None]))
    gl = jnp.exp(log_gamma * chunk)[None, :]
    wide = lambda t: jnp.repeat(t, RET_DV, axis=1)
    return decay, wide(read), wide(write), wide(gl)


def kernel(x_prompt, x_sample, cache_k, cache_v, state_ret, page_table, norm1_g, w_in, q_norm_g, k_norm_g,
           ret_norm_g, w_out, norm2_g, w_gate, w_up, w_down):
    nbat, s_len, d_model = x_prompt.shape
    db, n_q, _ = x_sample.shape
    depth = w_in.shape[0]
    n_pages = page_table.shape[1]
    past = n_pages * PAGE_SIZE
    nbf = past // MOBA_BLOCK
    assert n_pages % PAGES_PER_BLOCK == 0, "the new tokens must start a fresh MoBA block"
    assert nbf >= MOBA_TOPK and s_len % MOBA_BLOCK == 0 and s_len % RET_CHUNK == 0
    n_sel = MOBA_TOPK * PAGES_PER_BLOCK

    cache_k_t = jnp.transpose(cache_k, (0, 1, 3, 4, 2))
    cache_v_t = jnp.transpose(cache_v, (0, 1, 3, 4, 2))

    tabs_p = _rope_tables(jnp.arange(s_len))
    rc, rs, ac, as_ = _rope_tables(past + jnp.arange(n_q))
    tabs_s = (jnp.tile(rc, (db, 1)), jnp.tile(rs, (db, 1)), jnp.tile(ac, (1, db)), jnp.tile(as_, (1, db)))
    rtab_p = _retention_tables(RET_CHUNK)
    rtab_s = _retention_tables(n_q)

    blocks_per_step = 8 if nbf % 8 == 0 else 1
    kmean = _cache_kmean(cache_k_t, page_table, nbf, blocks_per_step)

    hp = x_prompt.reshape(nbat * s_len, d_model)
    hs = x_sample.reshape(db * n_q, d_model)
    zero_state = jnp.zeros((nbat, RET_HEADS, RET_DK, RET_DV), F32)
    tm_p = 512 if s_len % 512 == 0 else RET_CHUNK
    kp, vp, rp, kss, vss, rss = [], [], [], [], [], []
    d_split = 2 * RET_HEADS * RET_DK + 2 * D_RET
    for l in range(depth):
        ws = w_in[l, :, :d_split].astype(BF16)
        wt = w_in[l, :, d_split:].T.astype(BF16)
        g1 = norm1_g[l][None, :]
        g2 = norm2_g[l][None, :]
        qg = q_norm_g[l][:, None]
        kg = k_norm_g[l][:, None]
        ng = ret_norm_g[l].reshape(1, D_RET)
        wo = w_out[l].astype(BF16)
        wg = w_gate[l].astype(BF16)
        wu = w_up[l].astype(BF16)
        wd = w_down[l].astype(BF16)

        rq, rk, rv, rg, _, aqt, akt, avt, akb, avtb = _inproj(hp, g1, ws, wt, tabs_p, qg, kg, nbat, s_len, tm_p)
        yr, rst = _retention(rq, rk, rv, rg, zero_state, rtab_p, ng, nbat, s_len // RET_CHUNK, RET_CHUNK)
        ao = _moba_prompt(aqt, akt, akb, avtb, nbat * s_len)
        hp = _mix_ffn(yr, ao, hp, wo, g2, wg, wu, wd, tm_p)
        kp.append(akt)
        vp.append(avt)
        rp.append(rst)

        m_s = db * n_q
        rq, rk, rv, rg, aq, aqt, akt, avt, _, _ = _inproj(hs, g1, ws, wt, tabs_s, qg, kg, 1, m_s, m_s)
        yr, rst = _retention(rq, rk, rv, rg, state_ret[l], rtab_s, ng, db, 1, n_q)
        gidx = _sample_topk(aq, kmean[l])[..., :MOBA_TOPK].reshape(-1)
        per_seq = lambda t: t.reshape(ATT_HEADS, ATT_HD, db, n_q).transpose(2, 0, 1, 3)
        aot = _moba_sample(gidx, page_table, per_seq(aqt), per_seq(akt), per_seq(avt), cache_k_t, cache_v_t, l)
        ao = aot.transpose(0, 3, 1, 2).reshape(m_s, D_ATT)
        hs = _mix_ffn(yr, ao, hs, wo, g2, wg, wu, wd, m_s)
        rows = lambda t: t.reshape(D_ATT, db, n_q).transpose(1, 2, 0).reshape(db, n_q, ATT_HEADS, ATT_HD)
        kss.append(rows(akt))
        vss.append(rows(avt))
        rss.append(rst)

    seq_major = lambda ts: jnp.transpose(jnp.stack(ts), (0, 1, 4, 2, 3))
    return (hp.reshape(nbat, s_len, d_model), hs.reshape(db, n_q, d_model), seq_major(kp), seq_major(vp),
            jnp.stack(rp), jnp.stack(kss), jnp.stack(vss), jnp.stack(rss))
```

```python
import functools

import jax
import jax.numpy as jnp
from jax import lax
from jax.experimental import pallas as pl
from jax.experimental.pallas import tpu as pltpu

F32 = jnp.float32
BF16 = jnp.bfloat16

PAGE_SIZE = 128
RET_HEADS = 4
RET_DK = 128
RET_DV = 128
RET_CHUNK = 128
ATT_HEADS = 8
ATT_HD = 64
MOBA_BLOCK = 256
MOBA_TOPK = 3
Q_BLOCK = 128
ROPE_THETA = 10000.0
RMS_EPS = 1e-6
D_RET = RET_HEADS * RET_DV
D_ATT = ATT_HEADS * ATT_HD
PAGES_PER_BLOCK = MOBA_BLOCK // PAGE_SIZE

VMEM_LIMIT_BYTES = 56 * 1024 * 1024
NEG_INF = float("-inf")

_NT = (((1,), (1,)), ((), ()))
_TN = (((0,), (0,)), ((), ()))


def _params(*sem):
    return pltpu.CompilerParams(dimension_semantics=sem, vmem_limit_bytes=VMEM_LIMIT_BYTES)


def _const_spec(shape):
    nd = len(shape)
    return pl.BlockSpec(shape, lambda *_: (0,) * nd, pipeline_mode=pl.Buffered(1))


def _inproj_kernel(x_ref, g_ref, ws_ref, wt_ref, rcos_ref, rsin_ref, acos_ref, asin_ref, qg_ref, kg_ref,
                   rq_ref, rk_ref, rv_ref, rg_ref, q_ref, qt_ref, kt_ref, vt_ref, kb_ref, vtb_ref):
    x = x_ref[...]
    xn = (x * lax.rsqrt(jnp.mean(x * x, axis=-1, keepdims=True) + RMS_EPS) * g_ref[...]).astype(BF16)
    tm = x.shape[0]

    def std(c):
        return jnp.dot(xn, ws_ref[:, c * D_RET:(c + 1) * D_RET], preferred_element_type=F32)

    rcos = rcos_ref[...]
    rsin = rsin_ref[...]
    for c, o_ref, scale in ((0, rq_ref, None), (1, rk_ref, RET_DK ** -0.5)):
        acc = std(c)
        for h in range(RET_HEADS):
            xh = acc[:, h * RET_DK:(h + 1) * RET_DK]
            r = xh * rcos + pltpu.roll(xh, RET_DK // 2, 1) * rsin
            o_ref[:, h * RET_DK:(h + 1) * RET_DK] = r if scale is None else r * scale
    rv_ref[...] = std(2)
    rg_ref[...] = std(3)

    def tr(c):
        return lax.dot_general(wt_ref[c * D_ATT:(c + 1) * D_ATT, :], xn, _NT, preferred_element_type=F32)

    acos = acos_ref[...]
    asin = asin_ref[...]
    half = ATT_HD // 2
    for c, gn_ref, o_ref in ((0, qg_ref, qt_ref), (1, kg_ref, kt_ref)):
        acc = tr(c)
        for h in range(ATT_HEADS):
            xh = acc[h * ATT_HD:(h + 1) * ATT_HD, :]
            y = xh * lax.rsqrt(jnp.mean(xh * xh, axis=0, keepdims=True) + RMS_EPS) * gn_ref[...]
            y1 = y[:half]
            y2 = y[half:]
            o_ref[h, 0:half, :] = y1 * acos - y2 * asin
            o_ref[h, half:ATT_HD, :] = y2 * acos + y1 * asin
    q_ref[...] = qt_ref[...].reshape(D_ATT, tm).T
    kb_ref[...] = kt_ref[...].reshape(D_ATT, tm).T.astype(BF16)
    vt = tr(2).reshape(ATT_HEADS, ATT_HD, tm)
    vt_ref[...] = vt
    vtb_ref[...] = vt.astype(BF16)


def _inproj(x, g, ws, wt, tabs, qg, kg, nb, s_len, tm):
    m, d = x.shape
    nt = s_len // tm
    rcos, rsin, acos_t, asin_t = tabs
    row = lambda i: (i, 0)
    row_out = pl.BlockSpec((tm, D_RET), row)
    t_out = pl.BlockSpec((None, ATT_HEADS, ATT_HD, tm), lambda i: (i // nt, 0, 0, i % nt))
    t_shape = jax.ShapeDtypeStruct((nb, ATT_HEADS, ATT_HD, s_len), F32)
    r_shape = jax.ShapeDtypeStruct((m, D_RET), F32)
    return pl.pallas_call(
        _inproj_kernel,
        grid=(m // tm,),
        in_specs=[
            pl.BlockSpec((tm, d), row),
            _const_spec((1, d)),
            _const_spec(ws.shape),
            _const_spec(wt.shape),
            pl.BlockSpec((tm, RET_DK), lambda i: (i % nt, 0)),
            pl.BlockSpec((tm, RET_DK), lambda i: (i % nt, 0)),
            pl.BlockSpec((ATT_HD // 2, tm), lambda i: (0, i % nt)),
            pl.BlockSpec((ATT_HD // 2, tm), lambda i: (0, i % nt)),
            _const_spec((ATT_HD, 1)),
            _const_spec((ATT_HD, 1)),
        ],
        out_specs=[row_out, row_out, row_out, row_out, row_out, t_out, t_out, t_out, row_out, t_out],
        out_shape=[r_shape, r_shape, r_shape, r_shape, r_shape, t_shape, t_shape, t_shape,
                   jax.ShapeDtypeStruct((m, D_ATT), BF16), jax.ShapeDtypeStruct(t_shape.shape, BF16)],
        compiler_params=_params("arbitrary"),
        name="inproj",
    )(x, g, ws, wt, rcos, rsin, acos_t, asin_t, qg, kg)


def _retention_kernel(rq_ref, rk_ref, rv_ref, rg_ref, s0_ref, decay_ref, read_ref, write_ref, gl_ref, ng_ref,
                      y_ref, so_ref, st_ref):
    c = pl.program_id(1)

    @pl.when(c == 0)
    def _():
        st_ref[...] = s0_ref[...]

    sls = [slice(h * RET_DK, (h + 1) * RET_DK) for h in range(RET_HEADS)]
    qs = [rq_ref[:, sl].astype(BF16) for sl in sls]
    ks = [rk_ref[:, sl] for sl in sls]
    vs = [rv_ref[:, sl].astype(BF16) for sl in sls]
    states = [st_ref[h] for h in range(RET_HEADS)]
    scores = [lax.dot_general(qs[h], ks[h].astype(BF16), _NT, preferred_element_type=F32) for h in range(RET_HEADS)]
    cross = [jnp.dot(qs[h], states[h].astype(BF16), preferred_element_type=F32) for h in range(RET_HEADS)]
    for h in range(RET_HEADS):
        kw = (ks[h] * write_ref[:, sls[h]]).astype(BF16)
        st_ref[h] = gl_ref[:, sls[h]] * states[h] + lax.dot_general(kw, vs[h], _TN, preferred_element_type=F32)
    for h in range(RET_HEADS):
        s = (scores[h] * decay_ref[h]).astype(BF16)
        o = jnp.dot(s, vs[h], preferred_element_type=F32) + cross[h] * read_ref[:, sls[h]]
        y = o * lax.rsqrt(jnp.mean(o * o, axis=-1, keepdims=True) + RMS_EPS) * ng_ref[:, sls[h]]
        g = rg_ref[:, sls[h]]
        y_ref[:, sls[h]] = y * (g * jax.nn.sigmoid(g))

    @pl.when(c == pl.num_programs(1) - 1)
    def _():
        so_ref[...] = st_ref[...]


def _retention(rq, rk, rv, rg, state0, tabs, ng, nb, n_chunks, chunk):
    m = rq.shape[0]
    decay, read, write, gl = tabs
    row = pl.BlockSpec((chunk, D_RET), lambda b, c: (b * n_chunks + c, 0))
    st = pl.BlockSpec((None, RET_HEADS, RET_DK, RET_DV), lambda b, c: (b, 0, 0, 0))
    cs = lambda shape: pl.BlockSpec(shape, lambda b, c: (0,) * len(shape))
    return pl.pallas_call(
        _retention_kernel,
        grid=(nb, n_chunks),
        in_specs=[row, row, row, row, st, cs(decay.shape), cs(read.shape), cs(write.shape), cs(gl.shape), cs(ng.shape)],
        out_specs=[row, st],
        out_shape=[jax.ShapeDtypeStruct((m, D_RET), F32), jax.ShapeDtypeStruct(state0.shape, F32)],
        scratch_shapes=[pltpu.VMEM((RET_HEADS, RET_DK, RET_DV), F32)],
        compiler_params=_params("arbitrary", "arbitrary"),
        name="retention",
    )(rq, rk, rv, rg, state0, decay, read, write, gl, ng)


def _select_blocks(gate, n_past):
    nb = gate.shape[0]
    blk = lax.broadcasted_iota(jnp.int32, gate.shape, 0)
    valid = blk < n_past
    gm = jnp.where(valid, gate, NEG_INF)
    rank = jnp.zeros(gate.shape, jnp.int32)
    for m in range(nb):
        row = gm[m:m + 1, :]
        ahead = (row > gm) | ((row == gm) & (blk > m))
        rank = rank + ahead.astype(jnp.int32)
    return valid & (rank < MOBA_TOPK) & (jnp.abs(gate) < jnp.inf)


def _moba_prompt_kernel(qt_ref, kt_ref, kb_ref, vt_ref, o_ref, km_ref, bias_ref, s_ref, acc_ref):
    qi = pl.program_id(1)
    nb = kt_ref.shape[-1] // MOBA_BLOCK
    n_pairs = ATT_HEADS // 2

    @pl.when(qi == 0)
    def _():
        for h in range(ATT_HEADS):
            for n in range(nb):
                kblk = kt_ref[h, :, n * MOBA_BLOCK:(n + 1) * MOBA_BLOCK]
                mean = jnp.sum(kblk, axis=-1, keepdims=True) * (1.0 / MOBA_BLOCK)
                km_ref[h, n] = jnp.broadcast_to(mean, (ATT_HD, Q_BLOCK))

    ob = (qi * Q_BLOCK) // MOBA_BLOCK
    q_off = qi * Q_BLOCK - ob * MOBA_BLOCK

    q_pairs = []
    zero = jnp.zeros((ATT_HD, Q_BLOCK), BF16)
    for h in range(ATT_HEADS):
        qt = qt_ref[h]
        gate = jnp.concatenate(
            [jnp.sum(qt * km_ref[h, n], axis=0, keepdims=True) for n in range(nb)], axis=0)
        bias_ref[h] = jnp.where(_select_blocks(gate, ob), 0.0, NEG_INF).astype(F32)
        qs = (qt * ATT_HD ** -0.5).astype(BF16)
        if h % 2 == 0:
            top = jnp.concatenate([qs, zero], axis=1)
        else:
            q_pairs.append(jnp.concatenate([top, jnp.concatenate([zero, qs], axis=1)], axis=0))

    def scores(n, pair):
        start = pl.multiple_of(n * MOBA_BLOCK, MOBA_BLOCK)
        kb = kb_ref[pl.ds(start, MOBA_BLOCK), pair * 2 * ATT_HD:(pair + 1) * 2 * ATT_HD]
        return jnp.dot(kb, q_pairs[pair], preferred_element_type=F32)

    def values(n, h):
        start = pl.multiple_of(n * MOBA_BLOCK, MOBA_BLOCK)
        return vt_ref[h, :, pl.ds(start, MOBA_BLOCK)]

    key = lax.broadcasted_iota(jnp.int32, (MOBA_BLOCK, Q_BLOCK), 0)
    qry = lax.broadcasted_iota(jnp.int32, (MOBA_BLOCK, Q_BLOCK), 1)
    causal = jnp.where(key <= qry + q_off, 0.0, NEG_INF).astype(F32)

    def score_pass(n, masks, m_prev):
        m_new = []
        pair_scores = [scores(n, pair) for pair in range(n_pairs)]
        for pair in range(n_pairs):
            st = pair_scores[pair]
            for j in range(2):
                h = 2 * pair + j
                tile = st[:, j * Q_BLOCK:(j + 1) * Q_BLOCK] + masks[h]
                s_ref[n, h] = tile
                m_new.append(jnp.maximum(m_prev[h], jnp.max(tile, axis=0, keepdims=True)))
        return m_new

    m_own = score_pass(ob, [causal] * ATT_HEADS, [jnp.full((1, Q_BLOCK), NEG_INF, F32)] * ATT_HEADS)

    def past_scores(n, m_prev):
        return tuple(score_pass(n, [bias_ref[h, pl.ds(n, 1), :] for h in range(ATT_HEADS)], list(m_prev)))

    m_fin = lax.fori_loop(0, ob, past_scores, tuple(m_own))

    def value_pass(n, l_prev, acc_prev):
        ps = [jnp.exp(s_ref[n, h] - m_fin[h]) for h in range(ATT_HEADS)]
        l_new = [l_prev[h] + jnp.sum(ps[h], axis=0, keepdims=True) for h in range(ATT_HEADS)]
        acc_new = [acc_prev[h] + jnp.dot(values(n, h), ps[h].astype(BF16), preferred_element_type=F32)
                   for h in range(ATT_HEADS)]
        return l_new, acc_new

    l_own, acc_own = value_pass(ob, [jnp.zeros((1, Q_BLOCK), F32)] * ATT_HEADS,
                                [jnp.zeros((ATT_HD, Q_BLOCK), F32)] * ATT_HEADS)
    for h in range(ATT_HEADS):
        acc_ref[h] = acc_own[h]

    def past_values(n, l_prev):
        l_new, acc_new = value_pass(n, list(l_prev), [acc_ref[h] for h in range(ATT_HEADS)])
        for h in range(ATT_HEADS):
            acc_ref[h] = acc_new[h]
        return tuple(l_new)

    l_fin = lax.fori_loop(0, ob, past_values, tuple(l_own))
    out_t = jnp.concatenate([acc_ref[h] / l_fin[h] for h in range(ATT_HEADS)], axis=0)
    o_ref[...] = out_t.T


def _moba_prompt(qt, kt, kb, vtb, m_rows):
    nbat, _, _, s_len = kt.shape
    nqb = s_len // Q_BLOCK
    nblk = s_len // MOBA_BLOCK
    full = pl.BlockSpec((None, ATT_HEADS, ATT_HD, s_len), lambda b, i: (b, 0, 0, 0))
    return pl.pallas_call(
        _moba_prompt_kernel,
        grid=(nbat, nqb),
        in_specs=[
            pl.BlockSpec((None, ATT_HEADS, ATT_HD, Q_BLOCK), lambda b, i: (b, 0, 0, i)),
            full,
            pl.BlockSpec((s_len, D_ATT), lambda b, i: (b, 0)),
            full,
        ],
        out_specs=pl.BlockSpec((Q_BLOCK, D_ATT), lambda b, i: (b * nqb + i, 0)),
        out_shape=jax.ShapeDtypeStruct((m_rows, D_ATT), F32),
        scratch_shapes=[pltpu.VMEM((ATT_HEADS, nblk, ATT_HD, Q_BLOCK), F32),
                        pltpu.VMEM((ATT_HEADS, nblk, Q_BLOCK), F32),
                        pltpu.VMEM((nblk, ATT_HEADS, MOBA_BLOCK, Q_BLOCK), F32),
                        pltpu.VMEM((ATT_HEADS, ATT_HD, Q_BLOCK), F32)],
        compiler_params=_params("arbitrary", "arbitrary"),
        name="moba_prompt",
    )(qt, kt, kb, vtb)


def _kmean_kernel(pt_ref, *refs):
    del pt_ref
    page_refs, o_ref = refs[:-1], refs[-1]
    j = pl.program_id(2)
    n_per_step = len(page_refs) // PAGES_PER_BLOCK

    @pl.when(j == 0)
    def _():
        o_ref[...] = jnp.zeros(o_ref.shape, F32)

    lane = lax.broadcasted_iota(jnp.int32, o_ref.shape, 1)
    out = o_ref[...]
    for p in range(n_per_step):
        tot = page_refs[PAGES_PER_BLOCK * p][...]
        for r in range(1, PAGES_PER_BLOCK):
            tot = tot + page_refs[PAGES_PER_BLOCK * p + r][...]
        mean = jnp.sum(tot.reshape(D_ATT, PAGE_SIZE), axis=-1, keepdims=True) * (1.0 / MOBA_BLOCK)
        out = jnp.where(lane == j * n_per_step + p, mean, out)
    o_ref[...] = out


def _cache_kmean(cache_t, page_table, nbf, blocks_per_step):
    depth = cache_t.shape[0]
    db = page_table.shape[0]
    pages_per_step = blocks_per_step * PAGES_PER_BLOCK

    def page_spec(i):
        return pl.BlockSpec((None, None, ATT_HEADS, ATT_HD, PAGE_SIZE),
                            lambda l, b, j, pt: (l, pt[b, j * pages_per_step + i], 0, 0, 0))

    return pl.pallas_call(
        _kmean_kernel,
        grid_spec=pltpu.PrefetchScalarGridSpec(
            num_scalar_prefetch=1,
            grid=(depth, db, nbf // blocks_per_step),
            in_specs=[page_spec(i) for i in range(pages_per_step)],
            out_specs=pl.BlockSpec((None, None, D_ATT, nbf), lambda l, b, j, pt: (l, b, 0, 0)),
        ),
        out_shape=jax.ShapeDtypeStruct((depth, db, D_ATT, nbf), F32),
        compiler_params=_params("arbitrary", "arbitrary", "arbitrary"),
        name="cache_kmean",
    )(page_table, *([cache_t] * pages_per_step))


def _topk_kernel(q_ref, km_ref, idx_ref):
    nbf = km_ref.shape[-1]
    n_q = q_ref.shape[0]
    lane = lax.broadcasted_iota(jnp.int32, (n_q, nbf), 1).astype(F32)
    out_lane = lax.broadcasted_iota(jnp.int32, (n_q, 128), 1)
    for h in range(ATT_HEADS):
        q = q_ref[:, h * ATT_HD:(h + 1) * ATT_HD]
        gate = jnp.dot(q, km_ref[h * ATT_HD:(h + 1) * ATT_HD, :], precision=lax.Precision.HIGHEST,
                       preferred_element_type=F32)
        out = jnp.zeros((n_q, 128), jnp.int32)
        for t in range(MOBA_TOPK):
            best = jnp.max(gate, axis=-1, keepdims=True)
            idx = jnp.min(jnp.where(gate == best, lane, float(nbf)), axis=-1, keepdims=True)
            out = jnp.where(out_lane == t, idx.astype(jnp.int32), out)
            gate = jnp.where(lane == idx, NEG_INF, gate)
        idx_ref[h] = out


def _sample_topk(q, kmean_l):
    db, _, nbf = kmean_l.shape
    n_q = q.shape[0] // db
    return pl.pallas_call(
        _topk_kernel,
        grid=(db,),
        in_specs=[pl.BlockSpec((n_q, D_ATT), lambda b: (b, 0)),
                  pl.BlockSpec((None, D_ATT, nbf), lambda b: (b, 0, 0))],
        out_specs=pl.BlockSpec((None, ATT_HEADS, n_q, 128), lambda b: (b, 0, 0, 0)),
        out_shape=jax.ShapeDtypeStruct((db, ATT_HEADS, n_q, 128), jnp.int32),
        compiler_params=_params("arbitrary"),
        name="sample_topk",
    )(q, kmean_l)


def _moba_sample_kernel(gidx_ref, pt_ref, qt_ref, ktn_ref, vtn_ref, ck_hbm, cv_hbm, o_ref, kbuf, vbuf, sem, *, layer):
    n_heads = pl.num_programs(1)
    step = pl.program_id(0) * n_heads + pl.program_id(1)
    n_steps = pl.num_programs(0) * n_heads
    n_q = qt_ref.shape[-1]
    n_sel = MOBA_TOPK * PAGES_PER_BLOCK
    scale = ATT_HD ** -0.5

    def for_each_copy(step_, slot_, fn):
        b_ = step_ // n_heads
        h_ = step_ % n_heads
        base = step_ * (n_q * MOBA_TOPK)

        def body(g, _):
            blk = gidx_ref[base + g]
            for r in range(PAGES_PER_BLOCK):
                pid = pt_ref[b_, blk * PAGES_PER_BLOCK + r]
                i = g * PAGES_PER_BLOCK + r
                fn(pltpu.make_async_copy(ck_hbm.at[layer, pid, h_], kbuf.at[slot_, i], sem.at[slot_, 0]))
                fn(pltpu.make_async_copy(cv_hbm.at[layer, pid, h_], vbuf.at[slot_, i], sem.at[slot_, 1]))
            return 0

        lax.fori_loop(0, n_q * MOBA_TOPK, body, 0, unroll=4)

    def start_all(step_, slot_):
        for_each_copy(step_, slot_, lambda c: c.start())

    def wait_all(step_, slot_):
        for_each_copy(step_, slot_, lambda c: c.wait())

    slot = step % 2

    @pl.when(step == 0)
    def _():
        start_all(step, slot)

    @pl.when(step + 1 < n_steps)
    def _():
        start_all(step + 1, 1 - slot)

    wait_all(step, slot)

    qt = qt_ref[...]
    ktn = ktn_ref[...]
    vtn = vtn_ref[...]
    q_cols = [qt[:, l:l + 1] for l in range(n_q)]
    q_wide = [jnp.broadcast_to(qc, (ATT_HD, PAGE_SIZE)) for qc in q_cols]

    def rows(fn):
        return jnp.concatenate([fn(l) for l in range(n_q)], axis=0)

    s_new = rows(lambda l: jnp.sum(q_cols[l] * ktn, axis=0, keepdims=True)) * scale
    qry = lax.broadcasted_iota(jnp.int32, (n_q, n_q), 0)
    key = lax.broadcasted_iota(jnp.int32, (n_q, n_q), 1)
    s_new = jnp.where(key <= qry, s_new, NEG_INF)
    s_sel = [rows(lambda l: jnp.sum(q_wide[l] * kbuf[slot, l * n_sel + j], axis=0, keepdims=True)) * scale
             for j in range(n_sel)]
    m = jnp.max(s_new, axis=-1, keepdims=True)
    for s in s_sel:
        m = jnp.maximum(m, jnp.max(s, axis=-1, keepdims=True))
    p_new = jnp.exp(s_new - m)
    p_sel = [jnp.exp(s - m) for s in s_sel]
    denom = jnp.sum(p_new, axis=-1, keepdims=True)
    for p in p_sel:
        denom = denom + jnp.sum(p, axis=-1, keepdims=True)
    for l in range(n_q):
        acc = p_sel[0][l:l + 1, :] * vbuf[slot, l * n_sel]
        for j in range(1, n_sel):
            acc = acc + p_sel[j][l:l + 1, :] * vbuf[slot, l * n_sel + j]
        out = jnp.sum(acc, axis=-1, keepdims=True) + jnp.sum(p_new[l:l + 1, :] * vtn, axis=-1, keepdims=True)
        o_ref[:, l:l + 1] = out / denom[l:l + 1, :]


def _moba_sample(gidx, page_table, qt, ktn, vtn, cache_k_t, cache_v_t, layer):
    db, _, _, n_q = qt.shape
    n_pages = n_q * MOBA_TOPK * PAGES_PER_BLOCK
    small = pl.BlockSpec((None, None, ATT_HD, n_q), lambda b, h, gi, pt: (b, h, 0, 0))
    hbm = pl.BlockSpec(memory_space=pl.ANY)
    return pl.pallas_call(
        functools.partial(_moba_sample_kernel, layer=layer),
        grid_spec=pltpu.PrefetchScalarGridSpec(
            num_scalar_prefetch=2,
            grid=(db, ATT_HEADS),
            in_specs=[small, small, small, hbm, hbm],
            out_specs=small,
            scratch_shapes=[pltpu.VMEM((2, n_pages, ATT_HD, PAGE_SIZE), F32),
                            pltpu.VMEM((2, n_pages, ATT_HD, PAGE_SIZE), F32),
                            pltpu.SemaphoreType.DMA((2, 2))],
        ),
        out_shape=jax.ShapeDtypeStruct(qt.shape, F32),
        compiler_params=_params("arbitrary", "arbitrary"),
        name="moba_sample",
    )(gidx, page_table, qt, ktn, vtn, cache_k_t, cache_v_t)


def _mix_ffn_kernel(yr_ref, ao_ref, h_ref, wo_ref, g_ref, wg_ref, wu_ref, wd_ref, o_ref, *, ff_chunk):
    h1 = (h_ref[...]
          + jnp.dot(yr_ref[...].astype(BF16), wo_ref[0:D_RET, :], preferred_element_type=F32)
          + jnp.dot(ao_ref[...].astype(BF16), wo_ref[D_RET:D_RET + D_ATT, :], preferred_element_type=F32))
    hn = (h1 * lax.rsqrt(jnp.mean(h1 * h1, axis=-1, keepdims=True) + RMS_EPS) * g_ref[...]).astype(BF16)
    acc = jnp.zeros_like(h1)
    d_ff = wg_ref.shape[1]
    for c in range(d_ff // ff_chunk):
        sl = slice(c * ff_chunk, (c + 1) * ff_chunk)
        gate = jnp.dot(hn, wg_ref[:, sl], preferred_element_type=F32)
        up = jnp.dot(hn, wu_ref[:, sl], preferred_element_type=F32)
        act = (gate * jax.nn.sigmoid(gate) * up).astype(BF16)
        acc = acc + jnp.dot(act, wd_ref[sl, :], preferred_element_type=F32)
    o_ref[...] = h1 + acc


def _mix_ffn(yr, ao, h, wo, g, wg, wu, wd, tm):
    m, d = h.shape
    row = lambda w: pl.BlockSpec((tm, w), lambda i: (i, 0))
    return pl.pallas_call(
        functools.partial(_mix_ffn_kernel, ff_chunk=256),
        grid=(m // tm,),
        in_specs=[row(D_RET), row(D_ATT), row(d), _const_spec(wo.shape), _const_spec((1, d)),
                  _const_spec(wg.shape), _const_spec(wu.shape), _const_spec(wd.shape)],
        out_specs=row(d),
        out_shape=jax.ShapeDtypeStruct((m, d), F32),
        compiler_params=_params("arbitrary"),
        name="mix_ffn",
    )(yr, ao, h, wo, g, wg, wu, wd)


def _rope_tables(pos):
    ret_freq = 1.0 / (ROPE_THETA ** jnp.linspace(0.0, 1.0, RET_DK // 2, dtype=F32))
    att_freq = 1.0 / (ROPE_THETA ** (jnp.arange(0, ATT_HD, 2, dtype=F32) / ATT_HD))
    ang_r = pos.astype(F32)[:, None] * ret_freq[None, :]
    ang_a = pos.astype(F32)[:, None] * att_freq[None, :]
    rcos = jnp.concatenate([jnp.cos(ang_r), jnp.cos(ang_r)], axis=-1)
    rsin = jnp.concatenate([-jnp.sin(ang_r), jnp.sin(ang_r)], axis=-1)
    return rcos, rsin, jnp.cos(ang_a).T, jnp.sin(ang_a).T


def _retention_tables(chunk):
    log_gamma = jnp.log1p(-jnp.exp2(-5.0 - jnp.arange(RET_HEADS, dtype=F32)))
    i = jnp.arange(chunk, dtype=F32)
    diff = i[:, None] - i[None, :]
    decay = jnp.where(diff >= 0, jnp.exp(log_gamma[:, None, None] * jnp.maximum(diff, 0.0)), 0.0)
    read = jnp.exp(log_gamma[None, :] * (i[:, None] + 1.0))
    write = jnp.exp(log_gamma[None, :] * (chunk - 1.0 - i[:, None]))
    gl = jnp.exp(log_gamma * chunk)[None, :]
    wide = lambda t: jnp.repeat(t, RET_DV, axis=1)
    return decay, wide(read), wide(write), wide(gl)


def kernel(x_prompt, x_sample, cache_k, cache_v, state_ret, page_table, norm1_g, w_in, q_norm_g, k_norm_g,
           ret_norm_g, w_out, norm2_g, w_gate, w_up, w_down):
    nbat, s_len, d_model = x_prompt.shape
    db, n_q, _ = x_sample.shape
    depth = w_in.shape[0]
    n_pages = page_table.shape[1]
    past = n_pages * PAGE_SIZE
    nbf = past // MOBA_BLOCK
    assert n_pages % PAGES_PER_BLOCK == 0, "the new tokens must start a fresh MoBA block"
    assert nbf >= MOBA_TOPK and s_len % MOBA_BLOCK == 0 and s_len % RET_CHUNK == 0
    n_sel = MOBA_TOPK * PAGES_PER_BLOCK

    cache_k_t = jnp.transpose(cache_k, (0, 1, 3, 4, 2))
    cache_v_t = jnp.transpose(cache_v, (0, 1, 3, 4, 2))

    tabs_p = _rope_tables(jnp.arange(s_len))
    rc, rs, ac, as_ = _rope_tables(past + jnp.arange(n_q))
    tabs_s = (jnp.tile(rc, (db, 1)), jnp.tile(rs, (db, 1)), jnp.tile(ac, (1, db)), jnp.tile(as_, (1, db)))
    rtab_p = _retention_tables(RET_CHUNK)
    rtab_s = _retention_tables(n_q)

    blocks_per_step = 8 if nbf % 8 == 0 else 1
    kmean = _cache_kmean(cache_k_t, page_table, nbf, blocks_per_step)

    hp = x_prompt.reshape(nbat * s_len, d_model)
    hs = x_sample.reshape(db * n_q, d_model)
    zero_state = jnp.zeros((nbat, RET_HEADS, RET_DK, RET_DV), F32)
    tm_p = 512 if s_len % 512 == 0 else RET_CHUNK
    kp, vp, rp, kss, vss, rss = [], [], [], [], [], []
    d_split = 2 * RET_HEADS * RET_DK + 2 * D_RET
    for l in range(depth):
        ws = w_in[l, :, :d_split].astype(BF16)
        wt = w_in[l, :, d_split:].T.astype(BF16)
        g1 = norm1_g[l][None, :]
        g2 = norm2_g[l][None, :]
        qg = q_norm_g[l][:, None]
        kg = k_norm_g[l][:, None]
        ng = ret_norm_g[l].reshape(1, D_RET)
        wo = w_out[l].astype(BF16)
        wg = w_gate[l].astype(BF16)
        wu = w_up[l].astype(BF16)
        wd = w_down[l].astype(BF16)

        rq, rk, rv, rg, _, aqt, akt, avt, akb, avtb = _inproj(hp, g1, ws, wt, tabs_p, qg, kg, nbat, s_len, tm_p)
        yr, rst = _retention(rq, rk, rv, rg, zero_state, rtab_p, ng, nbat, s_len // RET_CHUNK, RET_CHUNK)
        ao = _moba_prompt(aqt, akt, akb, avtb, nbat * s_len)
        hp = _mix_ffn(yr, ao, hp, wo, g2, wg, wu, wd, tm_p)
        kp.append(akt)
        vp.append(avt)
        rp.append(rst)

        m_s = db * n_q
        rq, rk, rv, rg, aq, aqt, akt, avt, _, _ = _inproj(hs, g1, ws, wt, tabs_s, qg, kg, 1, m_s, m_s)
        yr, rst = _retention(rq, rk, rv, rg, state_ret[l], rtab_s, ng, db, 1, n_q)
        gidx = _sample_topk(aq, kmean[l])[..., :MOBA_TOPK].reshape(-1)
        per_seq = lambda t: t.reshape(ATT_HEADS, ATT_HD, db, n_q).transpose(2, 0, 1, 3)
        aot = _moba_sample(gidx, page_table, per_seq(aqt), per_seq(akt), per_seq(avt), cache_k_t, cache_v_t, l)
        ao = aot.transpose(0, 3, 1, 2).reshape(m_s, D_ATT)
        hs = _mix_ffn(yr, ao, hs, wo, g2, wg, wu, wd, m_s)
        rows = lambda t: t.reshape(D_ATT, db, n_q).transpose(1, 2, 0).reshape(db, n_q, ATT_HEADS, ATT_HD)
        kss.append(rows(akt))
        vss.append(rows(avt))
        rss.append(rst)

    seq_major = lambda ts: jnp.transpose(jnp.stack(ts), (0, 1, 4, 2, 3))
    return (hp.reshape(nbat, s_len, d_model), hs.reshape(db, n_q, d_model), seq_major(kp), seq_major(vp),
            jnp.stack(rp), jnp.stack(kss), jnp.stack(vss), jnp.stack(rss))
```

```python
import functools

import jax
import jax.numpy as jnp
from jax import lax
from jax.experimental import pallas as pl
from jax.experimental.pallas import tpu as pltpu

F32 = jnp.float32
BF16 = jnp.bfloat16

PAGE_SIZE = 128
RET_HEADS = 4
RET_DK = 128
RET_DV = 128
RET_CHUNK = 128
ATT_HEADS = 8
ATT_HD = 64
MOBA_BLOCK = 256
MOBA_TOPK = 3
Q_BLOCK = 128
ROPE_THETA = 10000.0
RMS_EPS = 1e-6
D_RET = RET_HEADS * RET_DV
D_ATT = ATT_HEADS * ATT_HD
PAGES_PER_BLOCK = MOBA_BLOCK // PAGE_SIZE

VMEM_LIMIT_BYTES = 56 * 1024 * 1024
NEG_INF = float("-inf")

_NT = (((1,), (1,)), ((), ()))
_TN = (((0,), (0,)), ((), ()))


def _params(*sem):
    return pltpu.CompilerParams(dimension_semantics=sem, vmem_limit_bytes=VMEM_LIMIT_BYTES)


def _const_spec(shape):
    nd = len(shape)
    return pl.BlockSpec(shape, lambda *_: (0,) * nd, pipeline_mode=pl.Buffered(1))


def _inproj_kernel(x_ref, g_ref, ws_ref, wt_ref, rcos_ref, rsin_ref, acos_ref, asin_ref, qg_ref, kg_ref,
                   rq_ref, rk_ref, rv_ref, rg_ref, q_ref, qt_ref, kt_ref, vt_ref, kb_ref, vtb_ref):
    x = x_ref[...]
    xn = (x * lax.rsqrt(jnp.mean(x * x, axis=-1, keepdims=True) + RMS_EPS) * g_ref[...]).astype(BF16)
    tm = x.shape[0]

    def std(c):
        return jnp.dot(xn, ws_ref[:, c * D_RET:(c + 1) * D_RET], preferred_element_type=F32)

    rcos = rcos_ref[...]
    rsin = rsin_ref[...]
    for c, o_ref, scale in ((0, rq_ref, None), (1, rk_ref, RET_DK ** -0.5)):
        acc = std(c)
        for h in range(RET_HEADS):
            xh = acc[:, h * RET_DK:(h + 1) * RET_DK]
            r = xh * rcos + pltpu.roll(xh, RET_DK // 2, 1) * rsin
            o_ref[:, h * RET_DK:(h + 1) * RET_DK] = r if scale is None else r * scale
    rv_ref[...] = std(2)
    rg_ref[...] = std(3)

    def tr(c):
        return lax.dot_general(wt_ref[c * D_ATT:(c + 1) * D_ATT, :], xn, _NT, preferred_element_type=F32)

    acos = acos_ref[...]
    asin = asin_ref[...]
    half = ATT_HD // 2
    for c, gn_ref, o_ref in ((0, qg_ref, qt_ref), (1, kg_ref, kt_ref)):
        acc = tr(c)
        for h in range(ATT_HEADS):
            xh = acc[h * ATT_HD:(h + 1) * ATT_HD, :]
            y = xh * lax.rsqrt(jnp.mean(xh * xh, axis=0, keepdims=True) + RMS_EPS) * gn_ref[...]
            y1 = y[:half]
            y2 = y[half:]
            o_ref[h, 0:half, :] = y1 * acos - y2 * asin
            o_ref[h, half:ATT_HD, :] = y2 * acos + y1 * asin
    q_ref[...] = qt_ref[...].reshape(D_ATT, tm).T
    kb_ref[...] = kt_ref[...].reshape(D_ATT, tm).T.astype(BF16)
    vt = tr(2).reshape(ATT_HEADS, ATT_HD, tm)
    vt_ref[...] = vt
    vtb_ref[...] = vt.astype(BF16)


def _inproj(x, g, ws, wt, tabs, qg, kg, nb, s_len, tm):
    m, d = x.shape
    nt = s_len // tm
    rcos, rsin, acos_t, asin_t = tabs
    row = lambda i: (i, 0)
    row_out = pl.BlockSpec((tm, D_RET), row)
    t_out = pl.BlockSpec((None, ATT_HEADS, ATT_HD, tm), lambda i: (i // nt, 0, 0, i % nt))
    t_shape = jax.ShapeDtypeStruct((nb, ATT_HEADS, ATT_HD, s_len), F32)
    r_shape = jax.ShapeDtypeStruct((m, D_RET), F32)
    return pl.pallas_call(
        _inproj_kernel,
        grid=(m // tm,),
        in_specs=[
            pl.BlockSpec((tm, d), row),
            _const_spec((1, d)),
            _const_spec(ws.shape),
            _const_spec(wt.shape),
            pl.BlockSpec((tm, RET_DK), lambda i: (i % nt, 0)),
            pl.BlockSpec((tm, RET_DK), lambda i: (i % nt, 0)),
            pl.BlockSpec((ATT_HD // 2, tm), lambda i: (0, i % nt)),
            pl.BlockSpec((ATT_HD // 2, tm), lambda i: (0, i % nt)),
            _const_spec((ATT_HD, 1)),
            _const_spec((ATT_HD, 1)),
        ],
        out_specs=[row_out, row_out, row_out, row_out, row_out, t_out, t_out, t_out, row_out, t_out],
        out_shape=[r_shape, r_shape, r_shape, r_shape, r_shape, t_shape, t_shape, t_shape,
                   jax.ShapeDtypeStruct((m, D_ATT), BF16), jax.ShapeDtypeStruct(t_shape.shape, BF16)],
        compiler_params=_params("arbitrary"),
        name="inproj",
    )(x, g, ws, wt, rcos, rsin, acos_t, asin_t, qg, kg)


def _retention_kernel(rq_ref, rk_ref, rv_ref, rg_ref, s0_ref, decay_ref, read_ref, write_ref, gl_ref, ng_ref,
                      y_ref, so_ref, st_ref):
    c = pl.program_id(1)

    @pl.when(c == 0)
    def _():
        st_ref[...] = s0_ref[...]

    chunk = decay_ref.shape[-1]
    sls = [slice(h * RET_DK, (h + 1) * RET_DK) for h in range(RET_HEADS)]
    states = [st_ref[h] for h in range(RET_HEADS)]
    for ci in range(rq_ref.shape[0] // chunk):
        rows = slice(ci * chunk, (ci + 1) * chunk)
        qs = [rq_ref[rows, sl].astype(BF16) for sl in sls]
        ks = [rk_ref[rows, sl] for sl in sls]
        vs = [rv_ref[rows, sl].astype(BF16) for sl in sls]
        scores = [lax.dot_general(qs[h], ks[h].astype(BF16), _NT, preferred_element_type=F32) for h in range(RET_HEADS)]
        cross = [jnp.dot(qs[h], states[h].astype(BF16), preferred_element_type=F32) for h in range(RET_HEADS)]
        for h in range(RET_HEADS):
            kw = (ks[h] * write_ref[:, sls[h]]).astype(BF16)
            states[h] = gl_ref[:, sls[h]] * states[h] + lax.dot_general(kw, vs[h], _TN, preferred_element_type=F32)
        for h in range(RET_HEADS):
            s = (scores[h] * decay_ref[h]).astype(BF16)
            o = jnp.dot(s, vs[h], preferred_element_type=F32) + cross[h] * read_ref[:, sls[h]]
            y = o * lax.rsqrt(jnp.mean(o * o, axis=-1, keepdims=True) + RMS_EPS) * ng_ref[:, sls[h]]
            g = rg_ref[rows, sls[h]]
            y_ref[rows, sls[h]] = y * (g * jax.nn.sigmoid(g))
    for h in range(RET_HEADS):
        st_ref[h] = states[h]

    @pl.when(c == pl.num_programs(1) - 1)
    def _():
        so_ref[...] = st_ref[...]


def _retention(rq, rk, rv, rg, state0, tabs, ng, nb, n_chunks, chunk):
    m = rq.shape[0]
    decay, read, write, gl = tabs
    per_step = next(c for c in (4, 2, 1) if n_chunks % c == 0)
    n_steps = n_chunks // per_step
    row = pl.BlockSpec((per_step * chunk, D_RET), lambda b, c: (b * n_steps + c, 0))
    st = pl.BlockSpec((None, RET_HEADS, RET_DK, RET_DV), lambda b, c: (b, 0, 0, 0))
    cs = lambda shape: pl.BlockSpec(shape, lambda b, c: (0,) * len(shape))
    return pl.pallas_call(
        _retention_kernel,
        grid=(nb, n_steps),
        in_specs=[row, row, row, row, st, cs(decay.shape), cs(read.shape), cs(write.shape), cs(gl.shape), cs(ng.shape)],
        out_specs=[row, st],
        out_shape=[jax.ShapeDtypeStruct((m, D_RET), F32), jax.ShapeDtypeStruct(state0.shape, F32)],
        scratch_shapes=[pltpu.VMEM((RET_HEADS, RET_DK, RET_DV), F32)],
        compiler_params=_params("arbitrary", "arbitrary"),
        name="retention",
    )(rq, rk, rv, rg, state0, decay, read, write, gl, ng)


def _select_blocks(gate, n_past):
    nb = gate.shape[0]
    blk = lax.broadcasted_iota(jnp.int32, gate.shape, 0)
    valid = blk < n_past
    gm = jnp.where(valid, gate, NEG_INF)
    rank = jnp.zeros(gate.shape, jnp.int32)
    for m in range(nb):
        row = gm[m:m + 1, :]
        ahead = (row > gm) | ((row == gm) & (blk > m))
        rank = rank + ahead.astype(jnp.int32)
    return valid & (rank < MOBA_TOPK) & (jnp.abs(gate) < jnp.inf)


def _moba_prompt_kernel(qt_ref, kt_ref, kb_ref, vt_ref, o_ref, km_ref, bias_ref, s_ref, acc_ref):
    qi = pl.program_id(1)
    nb = kt_ref.shape[-1] // MOBA_BLOCK
    n_pairs = ATT_HEADS // 2

    @pl.when(qi == 0)
    def _():
        for h in range(ATT_HEADS):
            for n in range(nb):
                kblk = kt_ref[h, :, n * MOBA_BLOCK:(n + 1) * MOBA_BLOCK]
                mean = jnp.sum(kblk, axis=-1, keepdims=True) * (1.0 / MOBA_BLOCK)
                km_ref[h, n] = jnp.broadcast_to(mean, (ATT_HD, Q_BLOCK))

    ob = (qi * Q_BLOCK) // MOBA_BLOCK
    q_off = qi * Q_BLOCK - ob * MOBA_BLOCK

    q_pairs = []
    zero = jnp.zeros((ATT_HD, Q_BLOCK), BF16)
    for h in range(ATT_HEADS):
        qt = qt_ref[h]
        gate = jnp.concatenate(
            [jnp.sum(qt * km_ref[h, n], axis=0, keepdims=True) for n in range(nb)], axis=0)
        bias_ref[h] = jnp.where(_select_blocks(gate, ob), 0.0, NEG_INF).astype(F32)
        qs = (qt * ATT_HD ** -0.5).astype(BF16)
        if h % 2 == 0:
            top = jnp.concatenate([qs, zero], axis=1)
        else:
            q_pairs.append(jnp.concatenate([top, jnp.concatenate([zero, qs], axis=1)], axis=0))

    def scores(n, pair):
        start = pl.multiple_of(n * MOBA_BLOCK, MOBA_BLOCK)
        kb = kb_ref[pl.ds(start, MOBA_BLOCK), pair * 2 * ATT_HD:(pair + 1) * 2 * ATT_HD]
        return jnp.dot(kb, q_pairs[pair], preferred_element_type=F32)

    def values(n, h):
        start = pl.multiple_of(n * MOBA_BLOCK, MOBA_BLOCK)
        return vt_ref[h, :, pl.ds(start, MOBA_BLOCK)]

    key = lax.broadcasted_iota(jnp.int32, (MOBA_BLOCK, Q_BLOCK), 0)
    qry = lax.broadcasted_iota(jnp.int32, (MOBA_BLOCK, Q_BLOCK), 1)
    causal = jnp.where(key <= qry + q_off, 0.0, NEG_INF).astype(F32)

    def score_pass(n, masks, m_prev):
        m_new = []
        pair_scores = [scores(n, pair) for pair in range(n_pairs)]
        for pair in range(n_pairs):
            st = pair_scores[pair]
            for j in range(2):
                h = 2 * pair + j
                tile = st[:, j * Q_BLOCK:(j + 1) * Q_BLOCK] + masks[h]
                s_ref[n, h] = tile
                m_new.append(jnp.maximum(m_prev[h], jnp.max(tile, axis=0, keepdims=True)))
        return m_new

    m_own = score_pass(ob, [causal] * ATT_HEADS, [jnp.full((1, Q_BLOCK), NEG_INF, F32)] * ATT_HEADS)

    def past_scores(n, m_prev):
        return tuple(score_pass(n, [bias_ref[h, pl.ds(n, 1), :] for h in range(ATT_HEADS)], list(m_prev)))

    m_fin = lax.fori_loop(0, ob, past_scores, tuple(m_own))

    def value_pass(n, l_prev, acc_prev):
        ps = [jnp.exp(s_ref[n, h] - m_fin[h]) for h in range(ATT_HEADS)]
        l_new = [l_prev[h] + jnp.sum(ps[h], axis=0, keepdims=True) for h in range(ATT_HEADS)]
        acc_new = [acc_prev[h] + jnp.dot(values(n, h), ps[h].astype(BF16), preferred_element_type=F32)
                   for h in range(ATT_HEADS)]
        return l_new, acc_new

    l_own, acc_own = value_pass(ob, [jnp.zeros((1, Q_BLOCK), F32)] * ATT_HEADS,
                                [jnp.zeros((ATT_HD, Q_BLOCK), F32)] * ATT_HEADS)
    for h in range(ATT_HEADS):
        acc_ref[h] = acc_own[h]

    def past_values(n, l_prev):
        l_new, acc_new = value_pass(n, list(l_prev), [acc_ref[h] for h in range(ATT_HEADS)])
        for h in range(ATT_HEADS):
            acc_ref[h] = acc_new[h]
        return tuple(l_new)

    l_fin = lax.fori_loop(0, ob, past_values, tuple(l_own))
    out_t = jnp.concatenate([acc_ref[h] / l_fin[h] for h in range(ATT_HEADS)], axis=0)
    o_ref[...] = out_t.T


def _moba_prompt(qt, kt, kb, vtb, m_rows):
    nbat, _, _, s_len = kt.shape
    nqb = s_len // Q_BLOCK
    nblk = s_len // MOBA_BLOCK
    full = pl.BlockSpec((None, ATT_HEADS, ATT_HD, s_len), lambda b, i: (b, 0, 0, 0))
    return pl.pallas_call(
        _moba_prompt_kernel,
        grid=(nbat, nqb),
        in_specs=[
            pl.BlockSpec((None, ATT_HEADS, ATT_HD, Q_BLOCK), lambda b, i: (b, 0, 0, i)),
            full,
            pl.BlockSpec((s_len, D_ATT), lambda b, i: (b, 0)),
            full,
        ],
        out_specs=pl.BlockSpec((Q_BLOCK, D_ATT), lambda b, i: (b * nqb + i, 0)),
        out_shape=jax.ShapeDtypeStruct((m_rows, D_ATT), F32),
        scratch_shapes=[pltpu.VMEM((ATT_HEADS, nblk, ATT_HD, Q_BLOCK), F32),
                        pltpu.VMEM((ATT_HEADS, nblk, Q_BLOCK), F32),
                        pltpu.VMEM((nblk, ATT_HEADS, MOBA_BLOCK, Q_BLOCK), F32),
                        pltpu.VMEM((ATT_HEADS, ATT_HD, Q_BLOCK), F32)],
        compiler_params=_params("arbitrary", "arbitrary"),
        name="moba_prompt",
    )(qt, kt, kb, vtb)


def _kmean_kernel(pt_ref, *refs):
    del pt_ref
    page_refs, o_ref = refs[:-1], refs[-1]
    j = pl.program_id(2)
    n_per_step = len(page_refs) // PAGES_PER_BLOCK

    @pl.when(j == 0)
    def _():
        o_ref[...] = jnp.zeros(o_ref.shape, F32)

    lane = lax.broadcasted_iota(jnp.int32, o_ref.shape, 1)
    out = o_ref[...]
    for p in range(n_per_step):
        tot = page_refs[PAGES_PER_BLOCK * p][...]
        for r in range(1, PAGES_PER_BLOCK):
            tot = tot + page_refs[PAGES_PER_BLOCK * p + r][...]
        mean = jnp.sum(tot.reshape(D_ATT, PAGE_SIZE), axis=-1, keepdims=True) * (1.0 / MOBA_BLOCK)
        out = jnp.where(lane == j * n_per_step + p, mean, out)
    o_ref[...] = out


def _cache_kmean(cache_t, page_table, nbf, blocks_per_step):
    depth = cache_t.shape[0]
    db = page_table.shape[0]
    pages_per_step = blocks_per_step * PAGES_PER_BLOCK

    def page_spec(i):
        return pl.BlockSpec((None, None, ATT_HEADS, ATT_HD, PAGE_SIZE),
                            lambda l, b, j, pt: (l, pt[b, j * pages_per_step + i], 0, 0, 0))

    return pl.pallas_call(
        _kmean_kernel,
        grid_spec=pltpu.PrefetchScalarGridSpec(
            num_scalar_prefetch=1,
            grid=(depth, db, nbf // blocks_per_step),
            in_specs=[page_spec(i) for i in range(pages_per_step)],
            out_specs=pl.BlockSpec((None, None, D_ATT, nbf), lambda l, b, j, pt: (l, b, 0, 0)),
        ),
        out_shape=jax.ShapeDtypeStruct((depth, db, D_ATT, nbf), F32),
        compiler_params=_params("arbitrary", "arbitrary", "arbitrary"),
        name="cache_kmean",
    )(page_table, *([cache_t] * pages_per_step))


def _topk_kernel(q_ref, km_ref, idx_ref):
    nbf = km_ref.shape[-1]
    n_q = q_ref.shape[0]
    lane = lax.broadcasted_iota(jnp.int32, (n_q, nbf), 1).astype(F32)
    out_lane = lax.broadcasted_iota(jnp.int32, (n_q, 128), 1)
    for h in range(ATT_HEADS):
        q = q_ref[:, h * ATT_HD:(h + 1) * ATT_HD]
        gate = jnp.dot(q, km_ref[h * ATT_HD:(h + 1) * ATT_HD, :], precision=lax.Precision.HIGHEST,
                       preferred_element_type=F32)
        out = jnp.zeros((n_q, 128), jnp.int32)
        for t in range(MOBA_TOPK):
            best = jnp.max(gate, axis=-1, keepdims=True)
            idx = jnp.min(jnp.where(gate == best, lane, float(nbf)), axis=-1, keepdims=True)
            out = jnp.where(out_lane == t, idx.astype(jnp.int32), out)
            gate = jnp.where(lane == idx, NEG_INF, gate)
        idx_ref[h] = out


def _sample_topk(q, kmean_l):
    db, _, nbf = kmean_l.shape
    n_q = q.shape[0] // db
    return pl.pallas_call(
        _topk_kernel,
        grid=(db,),
        in_specs=[pl.BlockSpec((n_q, D_ATT), lambda b: (b, 0)),
                  pl.BlockSpec((None, D_ATT, nbf), lambda b: (b, 0, 0))],
        out_specs=pl.BlockSpec((None, ATT_HEADS, n_q, 128), lambda b: (b, 0, 0, 0)),
        out_shape=jax.ShapeDtypeStruct((db, ATT_HEADS, n_q, 128), jnp.int32),
        compiler_params=_params("arbitrary"),
        name="sample_topk",
    )(q, kmean_l)


def _moba_sample_kernel(gidx_ref, pt_ref, qt_ref, ktn_ref, vtn_ref, ck_hbm, cv_hbm, o_ref, kbuf, vbuf, sem, *, layer):
    n_heads = pl.num_programs(1)
    step = pl.program_id(0) * n_heads + pl.program_id(1)
    n_steps = pl.num_programs(0) * n_heads
    n_q = qt_ref.shape[-1]
    n_sel = MOBA_TOPK * PAGES_PER_BLOCK
    scale = ATT_HD ** -0.5

    def for_each_copy(step_, slot_, fn):
        b_ = step_ // n_heads
        h_ = step_ % n_heads
        base = step_ * (n_q * MOBA_TOPK)

        def body(g, _):
            blk = gidx_ref[base + g]
            for r in range(PAGES_PER_BLOCK):
                pid = pt_ref[b_, blk * PAGES_PER_BLOCK + r]
                i = g * PAGES_PER_BLOCK + r
                fn(pltpu.make_async_copy(ck_hbm.at[layer, pid, h_], kbuf.at[slot_, i], sem.at[slot_, 0]))
                fn(pltpu.make_async_copy(cv_hbm.at[layer, pid, h_], vbuf.at[slot_, i], sem.at[slot_, 1]))
            return 0

        lax.fori_loop(0, n_q * MOBA_TOPK, body, 0, unroll=4)

    def start_all(step_, slot_):
        for_each_copy(step_, slot_, lambda c: c.start())

    def wait_all(step_, slot_):
        for_each_copy(step_, slot_, lambda c: c.wait())

    slot = step % 2

    @pl.when(step == 0)
    def _():
        start_all(step, slot)

    @pl.when(step + 1 < n_steps)
    def _():
        start_all(step + 1, 1 - slot)

    wait_all(step, slot)

    qt = qt_ref[...]
    ktn = ktn_ref[...]
    vtn = vtn_ref[...]
    q_cols = [qt[:, l:l + 1] for l in range(n_q)]
    q_wide = [jnp.broadcast_to(qc, (ATT_HD, PAGE_SIZE)) for qc in q_cols]

    def rows(fn):
        return jnp.concatenate([fn(l) for l in range(n_q)], axis=0)

    s_new = rows(lambda l: jnp.sum(q_cols[l] * ktn, axis=0, keepdims=True)) * scale
    qry = lax.broadcasted_iota(jnp.int32, (n_q, n_q), 0)
    key = lax.broadcasted_iota(jnp.int32, (n_q, n_q), 1)
    s_new = jnp.where(key <= qry, s_new, NEG_INF)
    s_sel = [rows(lambda l: jnp.sum(q_wide[l] * kbuf[slot, l * n_sel + j], axis=0, keepdims=True)) * scale
             for j in range(n_sel)]
    m = jnp.max(s_new, axis=-1, keepdims=True)
    for s in s_sel:
        m = jnp.maximum(m, jnp.max(s, axis=-1, keepdims=True))
    p_new = jnp.exp(s_new - m)
    p_sel = [jnp.exp(s - m) for s in s_sel]
    denom = jnp.sum(p_new, axis=-1, keepdims=True)
    for p in p_sel:
        denom = denom + jnp.sum(p, axis=-1, keepdims=True)
    for l in range(n_q):
        acc = p_sel[0][l:l + 1, :] * vbuf[slot, l * n_sel]
        for j in range(1, n_sel):
            acc = acc + p_sel[j][l:l + 1, :] * vbuf[slot, l * n_sel + j]
        out = jnp.sum(acc, axis=-1, keepdims=True) + jnp.sum(p_new[l:l + 1, :] * vtn, axis=-1, keepdims=True)
        o_ref[:, l:l + 1] = out / denom[l:l + 1, :]


def _moba_sample(gidx, page_table, qt, ktn, vtn, cache_k_t, cache_v_t, layer):
    db, _, _, n_q = qt.shape
    n_pages = n_q * MOBA_TOPK * PAGES_PER_BLOCK
    small = pl.BlockSpec((None, None, ATT_HD, n_q), lambda b, h, gi, pt: (b, h, 0, 0))
    hbm = pl.BlockSpec(memory_space=pl.ANY)
    return pl.pallas_call(
        functools.partial(_moba_sample_kernel, layer=layer),
        grid_spec=pltpu.PrefetchScalarGridSpec(
            num_scalar_prefetch=2,
            grid=(db, ATT_HEADS),
            in_specs=[small, small, small, hbm, hbm],
            out_specs=small,
            scratch_shapes=[pltpu.VMEM((2, n_pages, ATT_HD, PAGE_SIZE), F32),
                            pltpu.VMEM((2, n_pages, ATT_HD, PAGE_SIZE), F32),
                            pltpu.SemaphoreType.DMA((2, 2))],
        ),
        out_shape=jax.ShapeDtypeStruct(qt.shape, F32),
        compiler_params=_params("arbitrary", "arbitrary"),
        name="moba_sample",
    )(gidx, page_table, qt, ktn, vtn, cache_k_t, cache_v_t)


FF_CHUNK = 256


def _mix_ffn_body(yr_ref, ao_ref, h_ref, wo_ref, g_ref, wg_ref, wu_ref, wd_ref, o_ref, before_chunk=None):
    h1 = (h_ref[...]
          + jnp.dot(yr_ref[...].astype(BF16), wo_ref[0:D_RET, :], preferred_element_type=F32)
          + jnp.dot(ao_ref[...].astype(BF16), wo_ref[D_RET:D_RET + D_ATT, :], preferred_element_type=F32))
    hn = (h1 * lax.rsqrt(jnp.mean(h1 * h1, axis=-1, keepdims=True) + RMS_EPS) * g_ref[...]).astype(BF16)
    acc = jnp.zeros_like(h1)
    d_ff = wg_ref.shape[1]
    for c in range(d_ff // FF_CHUNK):
        if before_chunk is not None:
            before_chunk(c)
        sl = slice(c * FF_CHUNK, (c + 1) * FF_CHUNK)
        gate = jnp.dot(hn, wg_ref[:, sl], preferred_element_type=F32)
        up = jnp.dot(hn, wu_ref[:, sl], preferred_element_type=F32)
        act = (gate * jax.nn.sigmoid(gate) * up).astype(BF16)
        acc = acc + jnp.dot(act, wd_ref[sl, :], preferred_element_type=F32)
    o_ref[...] = h1 + acc


def _mix_ffn_kernel(yr_ref, ao_ref, h_ref, wo_ref, g_ref, wg_ref, wu_ref, wd_ref, o_ref):
    _mix_ffn_body(yr_ref, ao_ref, h_ref, wo_ref, g_ref, wg_ref, wu_ref, wd_ref, o_ref)


def _mix_ffn_kmean_kernel(pt_ref, yr_ref, ao_ref, h_ref, wo_ref, g_ref, wg_ref, wu_ref, wd_ref, ck_hbm,
                          o_ref, km_ref, pbuf, sem, *, layer, groups_per_tile, chunk_stride):
    i = pl.program_id(0)
    n_tiles = pl.num_programs(0)
    n_pages = pt_ref.shape[1]
    pages_per_group = pbuf.shape[1]
    blocks_per_group = pages_per_group // PAGES_PER_BLOCK
    tiles_per_seq = n_pages // (groups_per_tile * pages_per_group)
    b = i // tiles_per_seq
    t = i % tiles_per_seq

    def for_each_copy(group, fn):
        page0 = group * pages_per_group
        b_ = page0 // n_pages
        p0 = page0 % n_pages
        slot_ = group % 2
        for r in range(pages_per_group):
            fn(pltpu.make_async_copy(ck_hbm.at[layer, pt_ref[b_, p0 + r]], pbuf.at[slot_, r], sem.at[slot_]))

    @pl.when(i == 0)
    def _():
        for_each_copy(0, lambda c: c.start())

    @pl.when(t == 0)
    def _():
        km_ref[...] = jnp.zeros(km_ref.shape, F32)

    lane = lax.broadcasted_iota(jnp.int32, km_ref.shape, 1)

    def before_chunk(c):
        if c % chunk_stride != 0 or c // chunk_stride >= groups_per_tile:
            return
        q = c // chunk_stride
        group = i * groups_per_tile + q

        @pl.when(group + 1 < n_tiles * groups_per_tile)
        def _():
            for_each_copy(group + 1, lambda cp: cp.start())

        for_each_copy(group, lambda cp: cp.wait())
        slot = group % 2
        out = km_ref[...]
        for p in range(blocks_per_group):
            tot = pbuf[slot, PAGES_PER_BLOCK * p]
            for r in range(1, PAGES_PER_BLOCK):
                tot = tot + pbuf[slot, PAGES_PER_BLOCK * p + r]
            mean = jnp.sum(tot.reshape(D_ATT, PAGE_SIZE), axis=-1, keepdims=True) * (1.0 / MOBA_BLOCK)
            out = jnp.where(lane == (t * groups_per_tile + q) * blocks_per_group + p, mean, out)
        km_ref[...] = out

    _mix_ffn_body(yr_ref, ao_ref, h_ref, wo_ref, g_ref, wg_ref, wu_ref, wd_ref, o_ref, before_chunk)


def _mix_ffn(yr, ao, h, wo, g, wg, wu, wd, tm, kmean_stream=None):
    m, d = h.shape
    row = lambda w: pl.BlockSpec((tm, w), lambda i, *_: (i, 0))
    in_specs = [row(D_RET), row(D_ATT), row(d), _const_spec(wo.shape), _const_spec((1, d)),
                _const_spec(wg.shape), _const_spec(wu.shape), _const_spec(wd.shape)]
    out_shape = jax.ShapeDtypeStruct((m, d), F32)
    if kmean_stream is None:
        return pl.pallas_call(
            _mix_ffn_kernel, grid=(m // tm,), in_specs=in_specs, out_specs=row(d), out_shape=out_shape,
            compiler_params=_params("arbitrary"), name="mix_ffn",
        )(yr, ao, h, wo, g, wg, wu, wd)
    cache_k_t, page_table, layer, groups_per_tile = kmean_stream
    db, n_pages = page_table.shape
    n_tiles = m // tm
    pages_per_group = db * n_pages // (n_tiles * groups_per_tile)
    tiles_per_seq = n_tiles // db
    nbf = n_pages // PAGES_PER_BLOCK
    n_chunks = wg.shape[1] // FF_CHUNK
    return pl.pallas_call(
        functools.partial(_mix_ffn_kmean_kernel, layer=layer, groups_per_tile=groups_per_tile,
                          chunk_stride=n_chunks // groups_per_tile),
        grid_spec=pltpu.PrefetchScalarGridSpec(
            num_scalar_prefetch=1,
            grid=(n_tiles,),
            in_specs=in_specs + [pl.BlockSpec(memory_space=pl.ANY)],
            out_specs=[row(d), pl.BlockSpec((None, D_ATT, nbf), lambda i, pt: (i // tiles_per_seq, 0, 0))],
            scratch_shapes=[pltpu.VMEM((2, pages_per_group, ATT_HEADS, ATT_HD, PAGE_SIZE), F32),
                            pltpu.SemaphoreType.DMA((2,))],
        ),
        out_shape=[out_shape, jax.ShapeDtypeStruct((db, D_ATT, nbf), F32)],
        compiler_params=_params("arbitrary"),
        name="mix_ffn_kmean",
    )(page_table, yr, ao, h, wo, g, wg, wu, wd, cache_k_t)


def _rope_tables(pos):
    ret_freq = 1.0 / (ROPE_THETA ** jnp.linspace(0.0, 1.0, RET_DK // 2, dtype=F32))
    att_freq = 1.0 / (ROPE_THETA ** (jnp.arange(0, ATT_HD, 2, dtype=F32) / ATT_HD))
    ang_r = pos.astype(F32)[:, None] * ret_freq[None, :]
    ang_a = pos.astype(F32)[:, None] * att_freq[None, :]
    rcos = jnp.concatenate([jnp.cos(ang_r), jnp.cos(ang_r)], axis=-1)
    rsin = jnp.concatenate([-jnp.sin(ang_r), jnp.sin(ang_r)], axis=-1)
    return rcos, rsin, jnp.cos(ang_a).T, jnp.sin(ang_a).T


def _retention_tables(chunk):
    log_gamma = jnp.log1p(-jnp.exp2(-5.0 - jnp.arange(RET_HEADS, dtype=F32)))
    i = jnp.arange(chunk, dtype=F32)
    diff = i[:, None] - i[None, :]
    decay = jnp.where(diff >= 0, jnp.exp(log_gamma[:, None, None] * jnp.maximum(diff, 0.0)), 0.0)
    read = jnp.exp(log_gamma[None, :] * (i[:, None] + 1.0))
    write = jnp.exp(log_gamma[None, :] * (chunk - 1.0 - i[:, None]))
    gl = jnp.exp(log_gamma * chunk)[None, :]
    wide = lambda t: jnp.repeat(t, RET_DV, axis=1)
    return decay, wide(read), wide(write), wide(gl)


def kernel(x_prompt, x_sample, cache_k, cache_v, state_ret, page_table, norm1_g, w_in, q_norm_g, k_norm_g,
           ret_norm_g, w_out, norm2_g, w_gate, w_up, w_down):
    nbat, s_len, d_model = x_prompt.shape
    db, n_q, _ = x_sample.shape
    depth = w_in.shape[0]
    n_pages = page_table.shape[1]
    past = n_pages * PAGE_SIZE
    nbf = past // MOBA_BLOCK
    assert n_pages % PAGES_PER_BLOCK == 0, "the new tokens must start a fresh MoBA block"
    assert nbf >= MOBA_TOPK and s_len % MOBA_BLOCK == 0 and s_len % RET_CHUNK == 0
    n_sel = MOBA_TOPK * PAGES_PER_BLOCK

    cache_k_t = jnp.transpose(cache_k, (0, 1, 3, 4, 2))
    cache_v_t = jnp.transpose(cache_v, (0, 1, 3, 4, 2))

    tabs_p = _rope_tables(jnp.arange(s_len))
    rc, rs, ac, as_ = _rope_tables(past + jnp.arange(n_q))
    tabs_s = (jnp.tile(rc, (db, 1)), jnp.tile(rs, (db, 1)), jnp.tile(ac, (1, db)), jnp.tile(as_, (1, db)))
    rtab_p = _retention_tables(RET_CHUNK)
    rtab_s = _retention_tables(n_q)

    hp = x_prompt.reshape(nbat * s_len, d_model)
    hs = x_sample.reshape(db * n_q, d_model)
    zero_state = jnp.zeros((nbat, RET_HEADS, RET_DK, RET_DV), F32)
    tm_p = 512 if s_len % 512 == 0 else RET_CHUNK

    n_tiles = nbat * s_len // tm_p
    pages_per_tile = db * n_pages // n_tiles if (db * n_pages) % n_tiles == 0 else 0
    groups_per_tile = next((g for g in (4, 2, 1) if pages_per_tile and pages_per_tile % (g * PAGES_PER_BLOCK) == 0), 0)
    stream_kmean = groups_per_tile > 0 and n_tiles % db == 0 and n_pages % pages_per_tile == 0
    if not stream_kmean:
        kmean = _cache_kmean(cache_k_t, page_table, nbf, 8 if nbf % 8 == 0 else 1)
    kp, vp, rp, kss, vss, rss = [], [], [], [], [], []
    d_split = 2 * RET_HEADS * RET_DK + 2 * D_RET
    for l in range(depth):
        ws = w_in[l, :, :d_split].astype(BF16)
        wt = w_in[l, :, d_split:].T.astype(BF16)
        g1 = norm1_g[l][None, :]
        g2 = norm2_g[l][None, :]
        qg = q_norm_g[l][:, None]
        kg = k_norm_g[l][:, None]
        ng = ret_norm_g[l].reshape(1, D_RET)
        wo = w_out[l].astype(BF16)
        wg = w_gate[l].astype(BF16)
        wu = w_up[l].astype(BF16)
        wd = w_down[l].astype(BF16)

        rq, rk, rv, rg, _, aqt, akt, avt, akb, avtb = _inproj(hp, g1, ws, wt, tabs_p, qg, kg, nbat, s_len, tm_p)
        yr, rst = _retention(rq, rk, rv, rg, zero_state, rtab_p, ng, nbat, s_len // RET_CHUNK, RET_CHUNK)
        ao = _moba_prompt(aqt, akt, akb, avtb, nbat * s_len)
        if stream_kmean:
            hp, kmean_l = _mix_ffn(yr, ao, hp, wo, g2, wg, wu, wd, tm_p, (cache_k_t, page_table, l, groups_per_tile))
        else:
            hp, kmean_l = _mix_ffn(yr, ao, hp, wo, g2, wg, wu, wd, tm_p), kmean[l]
        kp.append(akt)
        vp.append(avt)
        rp.append(rst)

        m_s = db * n_q
        rq, rk, rv, rg, aq, aqt, akt, avt, _, _ = _inproj(hs, g1, ws, wt, tabs_s, qg, kg, 1, m_s, m_s)
        yr, rst = _retention(rq, rk, rv, rg, state_ret[l], rtab_s, ng, db, 1, n_q)
        gidx = _sample_topk(aq, kmean_l)[..., :MOBA_TOPK].reshape(-1)
        per_seq = lambda t: t.reshape(ATT_HEADS, ATT_HD, db, n_q).transpose(2, 0, 1, 3)
        aot = _moba_sample(gidx, page_table, per_seq(aqt), per_seq(akt), per_seq(avt), cache_k_t, cache_v_t, l)
        ao = aot.transpose(0, 3, 1, 2).reshape(m_s, D_ATT)
        hs = _mix_ffn(yr, ao, hs, wo, g2, wg, wu, wd, m_s)
        rows = lambda t: t.reshape(D_ATT, db, n_q).transpose(1, 2, 0).reshape(db, n_q, ATT_HEADS, ATT_HD)
        kss.append(rows(akt))
        vss.append(rows(avt))
        rss.append(rst)

    seq_major = lambda ts: jnp.transpose(jnp.stack(ts), (0, 1, 4, 2, 3))
    return (hp.reshape(nbat, s_len, d_model), hs.reshape(db, n_q, d_model), seq_major(kp), seq_major(vp),
            jnp.stack(rp), jnp.stack(kss), jnp.stack(vss), jnp.stack(rss))
```

```python
import functools

import jax
import jax.numpy as jnp
from jax import lax
from jax.experimental import pallas as pl
from jax.experimental.pallas import tpu as pltpu

F32 = jnp.float32
BF16 = jnp.bfloat16

PAGE_SIZE = 128
RET_HEADS = 4
RET_DK = 128
RET_DV = 128
RET_CHUNK = 128
ATT_HEADS = 8
ATT_HD = 64
MOBA_BLOCK = 256
MOBA_TOPK = 3
Q_BLOCK = 128
ROPE_THETA = 10000.0
RMS_EPS = 1e-6
D_RET = RET_HEADS * RET_DV
D_ATT = ATT_HEADS * ATT_HD
PAGES_PER_BLOCK = MOBA_BLOCK // PAGE_SIZE
SUM_ROWS = 16

VMEM_LIMIT_BYTES = 56 * 1024 * 1024
NEG_INF = float("-inf")

_NT = (((1,), (1,)), ((), ()))
_TN = (((0,), (0,)), ((), ()))


def _params(*sem):
    return pltpu.CompilerParams(dimension_semantics=sem, vmem_limit_bytes=VMEM_LIMIT_BYTES)


def _const_spec(shape):
    nd = len(shape)
    return pl.BlockSpec(shape, lambda *_: (0,) * nd, pipeline_mode=pl.Buffered(1))


def _project_tile(x_ref, g_ref, ws_ref, wt_ref, rcos_ref, rsin_ref, acos_ref, asin_ref, qg_ref, kg_ref,
                  rq_ref, rk_ref, rv_ref, rg_ref, qt_ref, kt_ref, vt_ref, q_ref=None, kb_ref=None, vtb_ref=None):
    x = x_ref[...]
    xn = (x * lax.rsqrt(jnp.mean(x * x, axis=-1, keepdims=True) + RMS_EPS) * g_ref[...]).astype(BF16)
    tm = x.shape[0]

    def std(c):
        return jnp.dot(xn, ws_ref[:, c * D_RET:(c + 1) * D_RET], preferred_element_type=F32)

    rcos = rcos_ref[...]
    rsin = rsin_ref[...]
    for c, o_ref, scale in ((0, rq_ref, None), (1, rk_ref, RET_DK ** -0.5)):
        acc = std(c)
        for h in range(RET_HEADS):
            xh = acc[:, h * RET_DK:(h + 1) * RET_DK]
            r = xh * rcos + pltpu.roll(xh, RET_DK // 2, 1) * rsin
            o_ref[:, h * RET_DK:(h + 1) * RET_DK] = r if scale is None else r * scale
    rv_ref[...] = std(2)
    rg_ref[...] = std(3)

    def tr(c):
        return lax.dot_general(wt_ref[c * D_ATT:(c + 1) * D_ATT, :], xn, _NT, preferred_element_type=F32)

    acos = acos_ref[...]
    asin = asin_ref[...]
    half = ATT_HD // 2
    for c, gn_ref, o_ref in ((0, qg_ref, qt_ref), (1, kg_ref, kt_ref)):
        acc = tr(c)
        for h in range(ATT_HEADS):
            xh = acc[h * ATT_HD:(h + 1) * ATT_HD, :]
            y = xh * lax.rsqrt(jnp.mean(xh * xh, axis=0, keepdims=True) + RMS_EPS) * gn_ref[...]
            y1 = y[:half]
            y2 = y[half:]
            o_ref[h, 0:half, :] = y1 * acos - y2 * asin
            o_ref[h, half:ATT_HD, :] = y2 * acos + y1 * asin
    if q_ref is not None:
        q_ref[...] = qt_ref[...].reshape(D_ATT, tm).T
    if kb_ref is not None:
        kb_ref[...] = kt_ref[...].reshape(D_ATT, tm).T.astype(BF16)
    vt = tr(2).reshape(ATT_HEADS, ATT_HD, tm)
    vt_ref[...] = vt
    if vtb_ref is not None:
        vtb_ref[...] = vt.astype(BF16)


def _retention_chunks(rq_ref, rk_ref, rv_ref, rg_ref, decay_ref, read_ref, write_ref, gl_ref, ng_ref, y_ref, st_ref):
    chunk = decay_ref.shape[-1]
    sls = [slice(h * RET_DK, (h + 1) * RET_DK) for h in range(RET_HEADS)]
    states = [st_ref[h] for h in range(RET_HEADS)]
    for ci in range(rq_ref.shape[0] // chunk):
        rows = slice(ci * chunk, (ci + 1) * chunk)
        qs = [rq_ref[rows, sl].astype(BF16) for sl in sls]
        ks = [rk_ref[rows, sl] for sl in sls]
        vs = [rv_ref[rows, sl].astype(BF16) for sl in sls]
        scores = [lax.dot_general(qs[h], ks[h].astype(BF16), _NT, preferred_element_type=F32) for h in range(RET_HEADS)]
        cross = [jnp.dot(qs[h], states[h].astype(BF16), preferred_element_type=F32) for h in range(RET_HEADS)]
        for h in range(RET_HEADS):
            kw = (ks[h] * write_ref[:, sls[h]]).astype(BF16)
            states[h] = gl_ref[:, sls[h]] * states[h] + lax.dot_general(kw, vs[h], _TN, preferred_element_type=F32)
        for h in range(RET_HEADS):
            s = (scores[h] * decay_ref[h]).astype(BF16)
            o = jnp.dot(s, vs[h], preferred_element_type=F32) + cross[h] * read_ref[:, sls[h]]
            y = o * lax.rsqrt(jnp.mean(o * o, axis=-1, keepdims=True) + RMS_EPS) * ng_ref[:, sls[h]]
            g = rg_ref[rows, sls[h]]
            y_ref[rows, sls[h]] = y * (g * jax.nn.sigmoid(g))
    for h in range(RET_HEADS):
        st_ref[h] = states[h]


def _inproj_kernel(x_ref, g_ref, ws_ref, wt_ref, rcos_ref, rsin_ref, acos_ref, asin_ref, qg_ref, kg_ref,
                   rq_ref, rk_ref, rv_ref, rg_ref, q_ref, qt_ref, kt_ref, vt_ref):
    _project_tile(x_ref, g_ref, ws_ref, wt_ref, rcos_ref, rsin_ref, acos_ref, asin_ref, qg_ref, kg_ref,
                  rq_ref, rk_ref, rv_ref, rg_ref, qt_ref, kt_ref, vt_ref, q_ref=q_ref)


def _inproj_retention_kernel(x_ref, g_ref, ws_ref, wt_ref, rcos_ref, rsin_ref, acos_ref, asin_ref, qg_ref, kg_ref,
                             decay_ref, read_ref, write_ref, gl_ref, ng_ref,
                             y_ref, so_ref, qt_ref, kt_ref, vt_ref, kb_ref, vtb_ref,
                             rq_ref, rk_ref, rv_ref, rg_ref, st_ref, *, tiles_per_seq):
    t = pl.program_id(0) % tiles_per_seq

    @pl.when(t == 0)
    def _():
        st_ref[...] = jnp.zeros(st_ref.shape, F32)

    _project_tile(x_ref, g_ref, ws_ref, wt_ref, rcos_ref, rsin_ref, acos_ref, asin_ref, qg_ref, kg_ref,
                  rq_ref, rk_ref, rv_ref, rg_ref, qt_ref, kt_ref, vt_ref, kb_ref=kb_ref, vtb_ref=vtb_ref)
    _retention_chunks(rq_ref, rk_ref, rv_ref, rg_ref, decay_ref, read_ref, write_ref, gl_ref, ng_ref, y_ref, st_ref)

    @pl.when(t == tiles_per_seq - 1)
    def _():
        so_ref[...] = st_ref[...]


def _inproj_specs(d, ws, wt, tm, nt):
    row = lambda i: (i, 0)
    return [
        pl.BlockSpec((tm, d), row),
        _const_spec((1, d)),
        _const_spec(ws.shape),
        _const_spec(wt.shape),
        pl.BlockSpec((tm, RET_DK), lambda i: (i % nt, 0)),
        pl.BlockSpec((tm, RET_DK), lambda i: (i % nt, 0)),
        pl.BlockSpec((ATT_HD // 2, tm), lambda i: (0, i % nt)),
        pl.BlockSpec((ATT_HD // 2, tm), lambda i: (0, i % nt)),
        _const_spec((ATT_HD, 1)),
        _const_spec((ATT_HD, 1)),
    ]


def _inproj(x, g, ws, wt, tabs, qg, kg, nb, s_len, tm):
    m, d = x.shape
    nt = s_len // tm
    row_out = pl.BlockSpec((tm, D_RET), lambda i: (i, 0))
    t_out = pl.BlockSpec((None, ATT_HEADS, ATT_HD, tm), lambda i: (i // nt, 0, 0, i % nt))
    t_shape = jax.ShapeDtypeStruct((nb, ATT_HEADS, ATT_HD, s_len), F32)
    r_shape = jax.ShapeDtypeStruct((m, D_RET), F32)
    return pl.pallas_call(
        _inproj_kernel,
        grid=(m // tm,),
        in_specs=_inproj_specs(d, ws, wt, tm, nt),
        out_specs=[row_out, row_out, row_out, row_out, row_out, t_out, t_out, t_out],
        out_shape=[r_shape, r_shape, r_shape, r_shape, r_shape, t_shape, t_shape, t_shape],
        compiler_params=_params("arbitrary"),
        name="inproj",
    )(x, g, ws, wt, *tabs, qg, kg)


def _inproj_retention(x, g, ws, wt, tabs, qg, kg, rtabs, ng, nb, s_len, tm):
    m, d = x.shape
    nt = s_len // tm
    decay, read, write, gl = rtabs
    row_out = pl.BlockSpec((tm, D_RET), lambda i: (i, 0))
    t_out = pl.BlockSpec((None, ATT_HEADS, ATT_HD, tm), lambda i: (i // nt, 0, 0, i % nt))
    st_out = pl.BlockSpec((None, RET_HEADS, RET_DK, RET_DV), lambda i: (i // nt, 0, 0, 0))
    t_shape = jax.ShapeDtypeStruct((nb, ATT_HEADS, ATT_HD, s_len), F32)
    tile = pltpu.VMEM((tm, D_RET), F32)
    return pl.pallas_call(
        functools.partial(_inproj_retention_kernel, tiles_per_seq=nt),
        grid=(m // tm,),
        in_specs=_inproj_specs(d, ws, wt, tm, nt) + [_const_spec(t.shape) for t in (decay, read, write, gl, ng)],
        out_specs=[row_out, st_out, t_out, t_out, t_out, row_out, t_out],
        out_shape=[jax.ShapeDtypeStruct((m, D_RET), F32),
                   jax.ShapeDtypeStruct((nb, RET_HEADS, RET_DK, RET_DV), F32),
                   t_shape, t_shape, t_shape,
                   jax.ShapeDtypeStruct((m, D_ATT), BF16), jax.ShapeDtypeStruct(t_shape.shape, BF16)],
        scratch_shapes=[tile, tile, tile, tile, pltpu.VMEM((RET_HEADS, RET_DK, RET_DV), F32)],
        compiler_params=_params("arbitrary"),
        name="inproj_retention",
    )(x, g, ws, wt, *tabs, qg, kg, decay, read, write, gl, ng)


def _retention_kernel(rq_ref, rk_ref, rv_ref, rg_ref, s0_ref, decay_ref, read_ref, write_ref, gl_ref, ng_ref,
                      y_ref, so_ref, st_ref):
    c = pl.program_id(1)

    @pl.when(c == 0)
    def _():
        st_ref[...] = s0_ref[...]

    _retention_chunks(rq_ref, rk_ref, rv_ref, rg_ref, decay_ref, read_ref, write_ref, gl_ref, ng_ref, y_ref, st_ref)

    @pl.when(c == pl.num_programs(1) - 1)
    def _():
        so_ref[...] = st_ref[...]


def _retention(rq, rk, rv, rg, state0, tabs, ng, nb, n_chunks, chunk):
    m = rq.shape[0]
    decay, read, write, gl = tabs
    per_step = next(c for c in (4, 2, 1) if n_chunks % c == 0)
    n_steps = n_chunks // per_step
    row = pl.BlockSpec((per_step * chunk, D_RET), lambda b, c: (b * n_steps + c, 0))
    st = pl.BlockSpec((None, RET_HEADS, RET_DK, RET_DV), lambda b, c: (b, 0, 0, 0))
    cs = lambda shape: pl.BlockSpec(shape, lambda b, c: (0,) * len(shape))
    return pl.pallas_call(
        _retention_kernel,
        grid=(nb, n_steps),
        in_specs=[row, row, row, row, st, cs(decay.shape), cs(read.shape), cs(write.shape), cs(gl.shape), cs(ng.shape)],
        out_specs=[row, st],
        out_shape=[jax.ShapeDtypeStruct((m, D_RET), F32), jax.ShapeDtypeStruct(state0.shape, F32)],
        scratch_shapes=[pltpu.VMEM((RET_HEADS, RET_DK, RET_DV), F32)],
        compiler_params=_params("arbitrary", "arbitrary"),
        name="retention",
    )(rq, rk, rv, rg, state0, decay, read, write, gl, ng)


def _select_blocks(gate, n_past):
    nb = gate.shape[0]
    blk = lax.broadcasted_iota(jnp.int32, gate.shape, 0)
    valid = blk < n_past
    gm = jnp.where(valid, gate, NEG_INF)
    rank = jnp.zeros(gate.shape, jnp.int32)
    for m in range(nb):
        row = gm[m:m + 1, :]
        ahead = (row > gm) | ((row == gm) & (blk > m))
        rank = rank + ahead.astype(jnp.int32)
    return valid & (rank < MOBA_TOPK) & (jnp.abs(gate) < jnp.inf)


def _moba_prompt_kernel(qt_ref, kt_ref, kb_ref, vt_ref, o_ref, km_ref, bias_ref, s_ref, acc_ref):
    qi = pl.program_id(1)
    nb = kt_ref.shape[-1] // MOBA_BLOCK
    n_pairs = ATT_HEADS // 2

    @pl.when(qi == 0)
    def _():
        for h in range(ATT_HEADS):
            for n in range(nb):
                kblk = kt_ref[h, :, n * MOBA_BLOCK:(n + 1) * MOBA_BLOCK]
                mean = jnp.sum(kblk, axis=-1, keepdims=True) * (1.0 / MOBA_BLOCK)
                km_ref[h, n] = jnp.broadcast_to(mean, (ATT_HD, Q_BLOCK))

    ob = (qi * Q_BLOCK) // MOBA_BLOCK
    q_off = qi * Q_BLOCK - ob * MOBA_BLOCK

    q_pairs = []
    zero = jnp.zeros((ATT_HD, Q_BLOCK), BF16)
    for h in range(ATT_HEADS):
        qt = qt_ref[h]
        gate = jnp.concatenate(
            [jnp.sum(qt * km_ref[h, n], axis=0, keepdims=True) for n in range(nb)], axis=0)
        bias_ref[h] = jnp.where(_select_blocks(gate, ob), 0.0, NEG_INF).astype(F32)
        qs = (qt * ATT_HD ** -0.5).astype(BF16)
        if h % 2 == 0:
            top = jnp.concatenate([qs, zero], axis=1)
        else:
            q_pairs.append(jnp.concatenate([top, jnp.concatenate([zero, qs], axis=1)], axis=0))

    def scores(n, pair):
        start = pl.multiple_of(n * MOBA_BLOCK, MOBA_BLOCK)
        kb = kb_ref[pl.ds(start, MOBA_BLOCK), pair * 2 * ATT_HD:(pair + 1) * 2 * ATT_HD]
        return jnp.dot(kb, q_pairs[pair], preferred_element_type=F32)

    def values(n, h):
        start = pl.multiple_of(n * MOBA_BLOCK, MOBA_BLOCK)
        return vt_ref[h, :, pl.ds(start, MOBA_BLOCK)]

    key = lax.broadcasted_iota(jnp.int32, (MOBA_BLOCK, Q_BLOCK), 0)
    qry = lax.broadcasted_iota(jnp.int32, (MOBA_BLOCK, Q_BLOCK), 1)
    causal = jnp.where(key <= qry + q_off, 0.0, NEG_INF).astype(F32)

    def score_pass(n, masks, m_prev):
        m_new = []
        pair_scores = [scores(n, pair) for pair in range(n_pairs)]
        for pair in range(n_pairs):
            st = pair_scores[pair]
            for j in range(2):
                h = 2 * pair + j
                tile = st[:, j * Q_BLOCK:(j + 1) * Q_BLOCK] + masks[h]
                s_ref[n, h] = tile
                m_new.append(jnp.maximum(m_prev[h], jnp.max(tile, axis=0, keepdims=True)))
        return m_new

    m_own = score_pass(ob, [causal] * ATT_HEADS, [jnp.full((1, Q_BLOCK), NEG_INF, F32)] * ATT_HEADS)

    def past_scores(n, m_prev):
        return tuple(score_pass(n, [bias_ref[h, pl.ds(n, 1), :] for h in range(ATT_HEADS)], list(m_prev)))

    m_fin = lax.fori_loop(0, ob, past_scores, tuple(m_own))

    ones_rows = jnp.ones((SUM_ROWS, MOBA_BLOCK), BF16)

    def value_pass(n, acc_prev):
        ps = [jnp.exp((s_ref[n, h] - m_fin[h]).astype(BF16)) for h in range(ATT_HEADS)]
        return [acc_prev[h] + jnp.dot(jnp.concatenate([values(n, h), ones_rows], axis=0), ps[h],
                                      preferred_element_type=F32) for h in range(ATT_HEADS)]

    acc_own = value_pass(ob, [jnp.zeros((ATT_HD + SUM_ROWS, Q_BLOCK), F32)] * ATT_HEADS)
    for h in range(ATT_HEADS):
        acc_ref[h] = acc_own[h]

    def past_values(n, _):
        acc_new = value_pass(n, [acc_ref[h] for h in range(ATT_HEADS)])
        for h in range(ATT_HEADS):
            acc_ref[h] = acc_new[h]
        return 0

    lax.fori_loop(0, ob, past_values, 0)
    out_t = jnp.concatenate([acc_ref[h, 0:ATT_HD, :] / acc_ref[h, ATT_HD:ATT_HD + 1, :] for h in range(ATT_HEADS)],
                            axis=0)
    o_ref[...] = out_t.T


def _moba_prompt(qt, kt, kb, vtb, m_rows):
    nbat, _, _, s_len = kt.shape
    nqb = s_len // Q_BLOCK
    nblk = s_len // MOBA_BLOCK
    full = pl.BlockSpec((None, ATT_HEADS, ATT_HD, s_len), lambda b, i: (b, 0, 0, 0))
    return pl.pallas_call(
        _moba_prompt_kernel,
        grid=(nbat, nqb),
        in_specs=[
            pl.BlockSpec((None, ATT_HEADS, ATT_HD, Q_BLOCK), lambda b, i: (b, 0, 0, i)),
            full,
            pl.BlockSpec((s_len, D_ATT), lambda b, i: (b, 0)),
            full,
        ],
        out_specs=pl.BlockSpec((Q_BLOCK, D_ATT), lambda b, i: (b * nqb + i, 0)),
        out_shape=jax.ShapeDtypeStruct((m_rows, D_ATT), F32),
        scratch_shapes=[pltpu.VMEM((ATT_HEADS, nblk, ATT_HD, Q_BLOCK), F32),
                        pltpu.VMEM((ATT_HEADS, nblk, Q_BLOCK), F32),
                        pltpu.VMEM((nblk, ATT_HEADS, MOBA_BLOCK, Q_BLOCK), F32),
                        pltpu.VMEM((ATT_HEADS, ATT_HD + SUM_ROWS, Q_BLOCK), F32)],
        compiler_params=_params("arbitrary", "arbitrary"),
        name="moba_prompt",
    )(qt, kt, kb, vtb)


def _kmean_kernel(pt_ref, *refs):
    del pt_ref
    page_refs, o_ref = refs[:-1], refs[-1]
    j = pl.program_id(2)
    n_per_step = len(page_refs) // PAGES_PER_BLOCK

    @pl.when(j == 0)
    def _():
        o_ref[...] = jnp.zeros(o_ref.shape, F32)

    lane = lax.broadcasted_iota(jnp.int32, o_ref.shape, 1)
    out = o_ref[...]
    for p in range(n_per_step):
        tot = page_refs[PAGES_PER_BLOCK * p][...]
        for r in range(1, PAGES_PER_BLOCK):
            tot = tot + page_refs[PAGES_PER_BLOCK * p + r][...]
        mean = jnp.sum(tot.reshape(D_ATT, PAGE_SIZE), axis=-1, keepdims=True) * (1.0 / MOBA_BLOCK)
        out = jnp.where(lane == j * n_per_step + p, mean, out)
    o_ref[...] = out


def _cache_kmean(cache_t, page_table, nbf, blocks_per_step):
    depth = cache_t.shape[0]
    db = page_table.shape[0]
    pages_per_step = blocks_per_step * PAGES_PER_BLOCK

    def page_spec(i):
        return pl.BlockSpec((None, None, ATT_HEADS, ATT_HD, PAGE_SIZE),
                            lambda l, b, j, pt: (l, pt[b, j * pages_per_step + i], 0, 0, 0))

    return pl.pallas_call(
        _kmean_kernel,
        grid_spec=pltpu.PrefetchScalarGridSpec(
            num_scalar_prefetch=1,
            grid=(depth, db, nbf // blocks_per_step),
            in_specs=[page_spec(i) for i in range(pages_per_step)],
            out_specs=pl.BlockSpec((None, None, D_ATT, nbf), lambda l, b, j, pt: (l, b, 0, 0)),
        ),
        out_shape=jax.ShapeDtypeStruct((depth, db, D_ATT, nbf), F32),
        compiler_params=_params("arbitrary", "arbitrary", "arbitrary"),
        name="cache_kmean",
    )(page_table, *([cache_t] * pages_per_step))


def _topk_kernel(q_ref, km_ref, idx_ref):
    nbf = km_ref.shape[-1]
    n_q = q_ref.shape[0]
    lane = lax.broadcasted_iota(jnp.int32, (n_q, nbf), 1).astype(F32)
    out_lane = lax.broadcasted_iota(jnp.int32, (n_q, 128), 1)
    for h in range(ATT_HEADS):
        q = q_ref[:, h * ATT_HD:(h + 1) * ATT_HD]
        gate = jnp.dot(q, km_ref[h * ATT_HD:(h + 1) * ATT_HD, :], precision=lax.Precision.HIGHEST,
                       preferred_element_type=F32)
        out = jnp.zeros((n_q, 128), jnp.int32)
        for t in range(MOBA_TOPK):
            best = jnp.max(gate, axis=-1, keepdims=True)
            idx = jnp.min(jnp.where(gate == best, lane, float(nbf)), axis=-1, keepdims=True)
            out = jnp.where(out_lane == t, idx.astype(jnp.int32), out)
            gate = jnp.where(lane == idx, NEG_INF, gate)
        idx_ref[h] = out


def _sample_topk(q, kmean_l):
    db, _, nbf = kmean_l.shape
    n_q = q.shape[0] // db
    return pl.pallas_call(
        _topk_kernel,
        grid=(db,),
        in_specs=[pl.BlockSpec((n_q, D_ATT), lambda b: (b, 0)),
                  pl.BlockSpec((None, D_ATT, nbf), lambda b: (b, 0, 0))],
        out_specs=pl.BlockSpec((None, ATT_HEADS, n_q, 128), lambda b: (b, 0, 0, 0)),
        out_shape=jax.ShapeDtypeStruct((db, ATT_HEADS, n_q, 128), jnp.int32),
        compiler_params=_params("arbitrary"),
        name="sample_topk",
    )(q, kmean_l)


def _moba_sample_kernel(gidx_ref, pt_ref, qt_ref, ktn_ref, vtn_ref, ck_hbm, cv_hbm, o_ref, kbuf, vbuf, sem, *, layer):
    n_heads = pl.num_programs(1)
    step = pl.program_id(0) * n_heads + pl.program_id(1)
    n_steps = pl.num_programs(0) * n_heads
    n_q = qt_ref.shape[-1]
    n_sel = MOBA_TOPK * PAGES_PER_BLOCK
    scale = ATT_HD ** -0.5

    def for_each_copy(step_, slot_, fn):
        b_ = step_ // n_heads
        h_ = step_ % n_heads
        base = step_ * (n_q * MOBA_TOPK)

        def body(g, _):
            blk = gidx_ref[base + g]
            for r in range(PAGES_PER_BLOCK):
                pid = pt_ref[b_, blk * PAGES_PER_BLOCK + r]
                i = g * PAGES_PER_BLOCK + r
                fn(pltpu.make_async_copy(ck_hbm.at[layer, pid, h_], kbuf.at[slot_, i], sem.at[slot_, 0]))
                fn(pltpu.make_async_copy(cv_hbm.at[layer, pid, h_], vbuf.at[slot_, i], sem.at[slot_, 1]))
            return 0

        lax.fori_loop(0, n_q * MOBA_TOPK, body, 0, unroll=4)

    def start_all(step_, slot_):
        for_each_copy(step_, slot_, lambda c: c.start())

    def wait_all(step_, slot_):
        for_each_copy(step_, slot_, lambda c: c.wait())

    slot = step % 2

    @pl.when(step == 0)
    def _():
        start_all(step, slot)

    @pl.when(step + 1 < n_steps)
    def _():
        start_all(step + 1, 1 - slot)

    wait_all(step, slot)

    qt = qt_ref[...]
    ktn = ktn_ref[...]
    vtn = vtn_ref[...]
    q_cols = [qt[:, l:l + 1] for l in range(n_q)]
    q_wide = [jnp.broadcast_to(qc, (ATT_HD, PAGE_SIZE)) for qc in q_cols]

    def rows(fn):
        return jnp.concatenate([fn(l) for l in range(n_q)], axis=0)

    s_new = rows(lambda l: jnp.sum(q_cols[l] * ktn, axis=0, keepdims=True)) * scale
    qry = lax.broadcasted_iota(jnp.int32, (n_q, n_q), 0)
    key = lax.broadcasted_iota(jnp.int32, (n_q, n_q), 1)
    s_new = jnp.where(key <= qry, s_new, NEG_INF)
    s_sel = [rows(lambda l: jnp.sum(q_wide[l] * kbuf[slot, l * n_sel + j], axis=0, keepdims=True)) * scale
             for j in range(n_sel)]
    m = jnp.max(s_new, axis=-1, keepdims=True)
    for s in s_sel:
        m = jnp.maximum(m, jnp.max(s, axis=-1, keepdims=True))
    p_new = jnp.exp(s_new - m)
    p_sel = [jnp.exp(s - m) for s in s_sel]
    denom = jnp.sum(p_new, axis=-1, keepdims=True)
    for p in p_sel:
        denom = denom + jnp.sum(p, axis=-1, keepdims=True)
    for l in range(n_q):
        acc = p_sel[0][l:l + 1, :] * vbuf[slot, l * n_sel]
        for j in range(1, n_sel):
            acc = acc + p_sel[j][l:l + 1, :] * vbuf[slot, l * n_sel + j]
        out = jnp.sum(acc, axis=-1, keepdims=True) + jnp.sum(p_new[l:l + 1, :] * vtn, axis=-1, keepdims=True)
        o_ref[:, l:l + 1] = out / denom[l:l + 1, :]


def _moba_sample(gidx, page_table, qt, ktn, vtn, cache_k_t, cache_v_t, layer):
    db, _, _, n_q = qt.shape
    n_pages = n_q * MOBA_TOPK * PAGES_PER_BLOCK
    small = pl.BlockSpec((None, None, ATT_HD, n_q), lambda b, h, gi, pt: (b, h, 0, 0))
    hbm = pl.BlockSpec(memory_space=pl.ANY)
    return pl.pallas_call(
        functools.partial(_moba_sample_kernel, layer=layer),
        grid_spec=pltpu.PrefetchScalarGridSpec(
            num_scalar_prefetch=2,
            grid=(db, ATT_HEADS),
            in_specs=[small, small, small, hbm, hbm],
            out_specs=small,
            scratch_shapes=[pltpu.VMEM((2, n_pages, ATT_HD, PAGE_SIZE), F32),
                            pltpu.VMEM((2, n_pages, ATT_HD, PAGE_SIZE), F32),
                            pltpu.SemaphoreType.DMA((2, 2))],
        ),
        out_shape=jax.ShapeDtypeStruct(qt.shape, F32),
        compiler_params=_params("arbitrary", "arbitrary"),
        name="moba_sample",
    )(gidx, page_table, qt, ktn, vtn, cache_k_t, cache_v_t)


FF_CHUNK = 256
KMEAN_SLOTS = 3


def _mix_ffn_body(yr_ref, ao_ref, h_ref, wo_ref, g_ref, wg_ref, wu_ref, wd_ref, o_ref, before_chunk=None):
    h1 = (h_ref[...]
          + jnp.dot(yr_ref[...].astype(BF16), wo_ref[0:D_RET, :], preferred_element_type=F32)
          + jnp.dot(ao_ref[...].astype(BF16), wo_ref[D_RET:D_RET + D_ATT, :], preferred_element_type=F32))
    hn = (h1 * lax.rsqrt(jnp.mean(h1 * h1, axis=-1, keepdims=True) + RMS_EPS) * g_ref[...]).astype(BF16)
    acc = jnp.zeros_like(h1)
    d_ff = wg_ref.shape[1]
    for c in range(d_ff // FF_CHUNK):
        if before_chunk is not None:
            before_chunk(c)
        sl = slice(c * FF_CHUNK, (c + 1) * FF_CHUNK)
        gate = jnp.dot(hn, wg_ref[:, sl], preferred_element_type=F32)
        up = jnp.dot(hn, wu_ref[:, sl], preferred_element_type=F32)
        act = (gate * jax.nn.sigmoid(gate) * up).astype(BF16)
        acc = acc + jnp.dot(act, wd_ref[sl, :], preferred_element_type=F32)
    o_ref[...] = h1 + acc


def _mix_ffn_kernel(yr_ref, ao_ref, h_ref, wo_ref, g_ref, wg_ref, wu_ref, wd_ref, o_ref):
    _mix_ffn_body(yr_ref, ao_ref, h_ref, wo_ref, g_ref, wg_ref, wu_ref, wd_ref, o_ref)


def _mix_ffn_kmean_kernel(pt_ref, yr_ref, ao_ref, h_ref, wo_ref, g_ref, wg_ref, wu_ref, wd_ref, ck_hbm,
                          o_ref, km_ref, pbuf, sem, *, layer, groups_per_tile, chunk_stride):
    i = pl.program_id(0)
    n_groups = pl.num_programs(0) * groups_per_tile
    n_pages = pt_ref.shape[1]
    pages_per_group = pbuf.shape[1]
    blocks_per_group = pages_per_group // PAGES_PER_BLOCK
    tiles_per_seq = n_pages // (groups_per_tile * pages_per_group)
    t = i % tiles_per_seq
    ahead = KMEAN_SLOTS - 1

    def for_each_copy(group, fn):
        page0 = group * pages_per_group
        b_ = page0 // n_pages
        p0 = page0 % n_pages
        slot_ = group % KMEAN_SLOTS
        for r in range(pages_per_group):
            fn(pltpu.make_async_copy(ck_hbm.at[layer, pt_ref[b_, p0 + r]], pbuf.at[slot_, r], sem.at[slot_]))

    @pl.when(i == 0)
    def _():
        for g in range(ahead):
            @pl.when(g < n_groups)
            def _():
                for_each_copy(g, lambda c: c.start())

    @pl.when(t == 0)
    def _():
        km_ref[...] = jnp.zeros(km_ref.shape, F32)

    lane = lax.broadcasted_iota(jnp.int32, km_ref.shape, 1)

    def before_chunk(c):
        if c % chunk_stride != 0 or c // chunk_stride >= groups_per_tile:
            return
        q = c // chunk_stride
        group = i * groups_per_tile + q

        @pl.when(group + ahead < n_groups)
        def _():
            for_each_copy(group + ahead, lambda cp: cp.start())

        for_each_copy(group, lambda cp: cp.wait())
        slot = group % KMEAN_SLOTS
        out = km_ref[...]
        for p in range(blocks_per_group):
            tot = pbuf[slot, PAGES_PER_BLOCK * p]
            for r in range(1, PAGES_PER_BLOCK):
                tot = tot + pbuf[slot, PAGES_PER_BLOCK * p + r]
            mean = jnp.sum(tot.reshape(D_ATT, PAGE_SIZE), axis=-1, keepdims=True) * (1.0 / MOBA_BLOCK)
            out = jnp.where(lane == (t * groups_per_tile + q) * blocks_per_group + p, mean, out)
        km_ref[...] = out

    _mix_ffn_body(yr_ref, ao_ref, h_ref, wo_ref, g_ref, wg_ref, wu_ref, wd_ref, o_ref, before_chunk)


def _mix_ffn(yr, ao, h, wo, g, wg, wu, wd, tm, kmean_stream=None):
    m, d = h.shape
    row = lambda w: pl.BlockSpec((tm, w), lambda i, *_: (i, 0))
    in_specs = [row(D_RET), row(D_ATT), row(d), _const_spec(wo.shape), _const_spec((1, d)),
                _const_spec(wg.shape), _const_spec(wu.shape), _const_spec(wd.shape)]
    out_shape = jax.ShapeDtypeStruct((m, d), F32)
    if kmean_stream is None:
        return pl.pallas_call(
            _mix_ffn_kernel, grid=(m // tm,), in_specs=in_specs, out_specs=row(d), out_shape=out_shape,
            compiler_params=_params("arbitrary"), name="mix_ffn",
        )(yr, ao, h, wo, g, wg, wu, wd)
    cache_k_t, page_table, layer, groups_per_tile = kmean_stream
    db, n_pages = page_table.shape
    n_tiles = m // tm
    pages_per_group = db * n_pages // (n_tiles * groups_per_tile)
    tiles_per_seq = n_tiles // db
    nbf = n_pages // PAGES_PER_BLOCK
    n_chunks = wg.shape[1] // FF_CHUNK
    return pl.pallas_call(
        functools.partial(_mix_ffn_kmean_kernel, layer=layer, groups_per_tile=groups_per_tile,
                          chunk_stride=n_chunks // groups_per_tile),
        grid_spec=pltpu.PrefetchScalarGridSpec(
            num_scalar_prefetch=1,
            grid=(n_tiles,),
            in_specs=in_specs + [pl.BlockSpec(memory_space=pl.ANY)],
            out_specs=[row(d), pl.BlockSpec((None, D_ATT, nbf), lambda i, pt: (i // tiles_per_seq, 0, 0))],
            scratch_shapes=[pltpu.VMEM((KMEAN_SLOTS, pages_per_group, ATT_HEADS, ATT_HD, PAGE_SIZE), F32),
                            pltpu.SemaphoreType.DMA((KMEAN_SLOTS,))],
        ),
        out_shape=[out_shape, jax.ShapeDtypeStruct((db, D_ATT, nbf), F32)],
        compiler_params=_params("arbitrary"),
        name="mix_ffn_kmean",
    )(page_table, yr, ao, h, wo, g, wg, wu, wd, cache_k_t)


def _rope_tables(pos):
    ret_freq = 1.0 / (ROPE_THETA ** jnp.linspace(0.0, 1.0, RET_DK // 2, dtype=F32))
    att_freq = 1.0 / (ROPE_THETA ** (jnp.arange(0, ATT_HD, 2, dtype=F32) / ATT_HD))
    ang_r = pos.astype(F32)[:, None] * ret_freq[None, :]
    ang_a = pos.astype(F32)[:, None] * att_freq[None, :]
    rcos = jnp.concatenate([jnp.cos(ang_r), jnp.cos(ang_r)], axis=-1)
    rsin = jnp.concatenate([-jnp.sin(ang_r), jnp.sin(ang_r)], axis=-1)
    return rcos, rsin, jnp.cos(ang_a).T, jnp.sin(ang_a).T


def _retention_tables(chunk):
    log_gamma = jnp.log1p(-jnp.exp2(-5.0 - jnp.arange(RET_HEADS, dtype=F32)))
    i = jnp.arange(chunk, dtype=F32)
    diff = i[:, None] - i[None, :]
    decay = jnp.where(diff >= 0, jnp.exp(log_gamma[:, None, None] * jnp.maximum(diff, 0.0)), 0.0)
    read = jnp.exp(log_gamma[None, :] * (i[:, None] + 1.0))
    write = jnp.exp(log_gamma[None, :] * (chunk - 1.0 - i[:, None]))
    gl = jnp.exp(log_gamma * chunk)[None, :]
    wide = lambda t: jnp.repeat(t, RET_DV, axis=1)
    return decay, wide(read), wide(write), wide(gl)


def kernel(x_prompt, x_sample, cache_k, cache_v, state_ret, page_table, norm1_g, w_in, q_norm_g, k_norm_g,
           ret_norm_g, w_out, norm2_g, w_gate, w_up, w_down):
    nbat, s_len, d_model = x_prompt.shape
    db, n_q, _ = x_sample.shape
    depth = w_in.shape[0]
    n_pages = page_table.shape[1]
    past = n_pages * PAGE_SIZE
    nbf = past // MOBA_BLOCK
    assert n_pages % PAGES_PER_BLOCK == 0, "the new tokens must start a fresh MoBA block"
    assert nbf >= MOBA_TOPK and s_len % MOBA_BLOCK == 0 and s_len % RET_CHUNK == 0

    cache_k_t = jnp.transpose(cache_k, (0, 1, 3, 4, 2))
    cache_v_t = jnp.transpose(cache_v, (0, 1, 3, 4, 2))

    tabs_p = _rope_tables(jnp.arange(s_len))
    rc, rs, ac, as_ = _rope_tables(past + jnp.arange(n_q))
    tabs_s = (jnp.tile(rc, (db, 1)), jnp.tile(rs, (db, 1)), jnp.tile(ac, (1, db)), jnp.tile(as_, (1, db)))
    rtab_p = _retention_tables(RET_CHUNK)
    rtab_s = _retention_tables(n_q)

    hp = x_prompt.reshape(nbat * s_len, d_model)
    hs = x_sample.reshape(db * n_q, d_model)
    tm_p = 512 if s_len % 512 == 0 else RET_CHUNK

    n_tiles = nbat * s_len // tm_p
    pages_per_tile = db * n_pages // n_tiles if (db * n_pages) % n_tiles == 0 else 0
    groups_per_tile = next((g for g in (4, 2, 1) if pages_per_tile and pages_per_tile % (g * PAGES_PER_BLOCK) == 0), 0)
    stream_kmean = groups_per_tile > 0 and n_tiles % db == 0 and n_pages % pages_per_tile == 0
    if not stream_kmean:
        kmean = _cache_kmean(cache_k_t, page_table, nbf, 8 if nbf % 8 == 0 else 1)
    kp, vp, rp, kss, vss, rss = [], [], [], [], [], []
    d_split = 2 * RET_HEADS * RET_DK + 2 * D_RET
    for l in range(depth):
        ws = w_in[l, :, :d_split].astype(BF16)
        wt = w_in[l, :, d_split:].T.astype(BF16)
        g1 = norm1_g[l][None, :]
        g2 = norm2_g[l][None, :]
        qg = q_norm_g[l][:, None]
        kg = k_norm_g[l][:, None]
        ng = ret_norm_g[l].reshape(1, D_RET)
        wo = w_out[l].astype(BF16)
        wg = w_gate[l].astype(BF16)
        wu = w_up[l].astype(BF16)
        wd = w_down[l].astype(BF16)

        yr, rst, aqt, akt, avt, akb, avtb = _inproj_retention(hp, g1, ws, wt, tabs_p, qg, kg, rtab_p, ng, nbat, s_len, tm_p)
        ao = _moba_prompt(aqt, akt, akb, avtb, nbat * s_len)
        if stream_kmean:
            hp, kmean_l = _mix_ffn(yr, ao, hp, wo, g2, wg, wu, wd, tm_p, (cache_k_t, page_table, l, groups_per_tile))
        else:
            hp, kmean_l = _mix_ffn(yr, ao, hp, wo, g2, wg, wu, wd, tm_p), kmean[l]
        kp.append(akt)
        vp.append(avt)
        rp.append(rst)

        m_s = db * n_q
        rq, rk, rv, rg, aq, aqt, akt, avt = _inproj(hs, g1, ws, wt, tabs_s, qg, kg, 1, m_s, m_s)
        yr, rst = _retention(rq, rk, rv, rg, state_ret[l], rtab_s, ng, db, 1, n_q)
        gidx = _sample_topk(aq, kmean_l)[..., :MOBA_TOPK].reshape(-1)
        per_seq = lambda t: t.reshape(ATT_HEADS, ATT_HD, db, n_q).transpose(2, 0, 1, 3)
        aot = _moba_sample(gidx, page_table, per_seq(aqt), per_seq(akt), per_seq(avt), cache_k_t, cache_v_t, l)
        ao = aot.transpose(0, 3, 1, 2).reshape(m_s, D_ATT)
        hs = _mix_ffn(yr, ao, hs, wo, g2, wg, wu, wd, m_s)
        rows = lambda t: t.reshape(D_ATT, db, n_q).transpose(1, 2, 0).reshape(db, n_q, ATT_HEADS, ATT_HD)
        kss.append(rows(akt))
        vss.append(rows(avt))
        rss.append(rst)

    seq_major = lambda ts: jnp.transpose(jnp.stack(ts), (0, 1, 4, 2, 3))
    return (hp.reshape(nbat, s_len, d_model), hs.reshape(db, n_q, d_model), seq_major(kp), seq_major(vp),
            jnp.stack(rp), jnp.stack(kss), jnp.stack(vss), jnp.stack(rss))
```

```python
import functools

import jax
import jax.numpy as jnp
from jax import lax
from jax.experimental import pallas as pl
from jax.experimental.pallas import tpu as pltpu

F32 = jnp.float32
BF16 = jnp.bfloat16

PAGE_SIZE = 128
RET_HEADS = 4
RET_DK = 128
RET_DV = 128
RET_CHUNK = 128
ATT_HEADS = 8
ATT_HD = 64
MOBA_BLOCK = 256
MOBA_TOPK = 3
Q_BLOCK = 256
ROPE_THETA = 10000.0
RMS_EPS = 1e-6
D_RET = RET_HEADS * RET_DV
D_ATT = ATT_HEADS * ATT_HD
PAGES_PER_BLOCK = MOBA_BLOCK // PAGE_SIZE
SUM_ROWS = 16

VMEM_LIMIT_BYTES = 56 * 1024 * 1024
NEG_INF = float("-inf")

_NT = (((1,), (1,)), ((), ()))
_TN = (((0,), (0,)), ((), ()))


def _params(*sem):
    return pltpu.CompilerParams(dimension_semantics=sem, vmem_limit_bytes=VMEM_LIMIT_BYTES)


def _const_spec(shape):
    nd = len(shape)
    return pl.BlockSpec(shape, lambda *_: (0,) * nd, pipeline_mode=pl.Buffered(1))


def _project_tile(x_ref, g_ref, ws_ref, wt_ref, rcos_ref, rsin_ref, acos_ref, asin_ref, qg_ref, kg_ref,
                  rq_ref, rk_ref, rv_ref, rg_ref, qt_ref, kt_ref, vt_ref, q_ref=None, kb_ref=None, vtb_ref=None):
    x = x_ref[...]
    xn = (x * lax.rsqrt(jnp.mean(x * x, axis=-1, keepdims=True) + RMS_EPS) * g_ref[...]).astype(BF16)
    tm = x.shape[0]

    def std(c):
        return jnp.dot(xn, ws_ref[:, c * D_RET:(c + 1) * D_RET], preferred_element_type=F32)

    rcos = rcos_ref[...]
    rsin = rsin_ref[...]
    for c, o_ref, scale in ((0, rq_ref, None), (1, rk_ref, RET_DK ** -0.5)):
        acc = std(c)
        for h in range(RET_HEADS):
            xh = acc[:, h * RET_DK:(h + 1) * RET_DK]
            r = xh * rcos + pltpu.roll(xh, RET_DK // 2, 1) * rsin
            o_ref[:, h * RET_DK:(h + 1) * RET_DK] = r if scale is None else r * scale
    rv_ref[...] = std(2)
    rg_ref[...] = std(3)

    def tr(c):
        return lax.dot_general(wt_ref[c * D_ATT:(c + 1) * D_ATT, :], xn, _NT, preferred_element_type=F32)

    acos = acos_ref[...]
    asin = asin_ref[...]
    half = ATT_HD // 2
    for c, gn_ref, o_ref in ((0, qg_ref, qt_ref), (1, kg_ref, kt_ref)):
        acc = tr(c)
        for h in range(ATT_HEADS):
            xh = acc[h * ATT_HD:(h + 1) * ATT_HD, :]
            y = xh * lax.rsqrt(jnp.mean(xh * xh, axis=0, keepdims=True) + RMS_EPS) * gn_ref[...]
            y1 = y[:half]
            y2 = y[half:]
            o_ref[h, 0:half, :] = y1 * acos - y2 * asin
            o_ref[h, half:ATT_HD, :] = y2 * acos + y1 * asin
    if q_ref is not None:
        q_ref[...] = qt_ref[...].reshape(D_ATT, tm).T
    if kb_ref is not None:
        kb_ref[...] = kt_ref[...].reshape(D_ATT, tm).T.astype(BF16)
    vt = tr(2).reshape(ATT_HEADS, ATT_HD, tm)
    vt_ref[...] = vt
    if vtb_ref is not None:
        vtb_ref[...] = vt.astype(BF16)


def _retention_chunks(rq_ref, rk_ref, rv_ref, rg_ref, decay_ref, read_ref, write_ref, gl_ref, ng_ref, y_ref, st_ref):
    chunk = decay_ref.shape[-1]
    sls = [slice(h * RET_DK, (h + 1) * RET_DK) for h in range(RET_HEADS)]
    states = [st_ref[h] for h in range(RET_HEADS)]
    for ci in range(rq_ref.shape[0] // chunk):
        rows = slice(ci * chunk, (ci + 1) * chunk)
        qs = [rq_ref[rows, sl].astype(BF16) for sl in sls]
        ks = [rk_ref[rows, sl] for sl in sls]
        vs = [rv_ref[rows, sl].astype(BF16) for sl in sls]
        scores = [lax.dot_general(qs[h], ks[h].astype(BF16), _NT, preferred_element_type=F32) for h in range(RET_HEADS)]
        cross = [jnp.dot(qs[h], states[h].astype(BF16), preferred_element_type=F32) for h in range(RET_HEADS)]
        for h in range(RET_HEADS):
            kw = (ks[h] * write_ref[:, sls[h]]).astype(BF16)
            states[h] = gl_ref[:, sls[h]] * states[h] + lax.dot_general(kw, vs[h], _TN, preferred_element_type=F32)
        for h in range(RET_HEADS):
            s = (scores[h] * decay_ref[h]).astype(BF16)
            o = jnp.dot(s, vs[h], preferred_element_type=F32) + cross[h] * read_ref[:, sls[h]]
            y = o * lax.rsqrt(jnp.mean(o * o, axis=-1, keepdims=True) + RMS_EPS) * ng_ref[:, sls[h]]
            g = rg_ref[rows, sls[h]]
            y_ref[rows, sls[h]] = y * (g * jax.nn.sigmoid(g))
    for h in range(RET_HEADS):
        st_ref[h] = states[h]


def _inproj_kernel(x_ref, g_ref, ws_ref, wt_ref, rcos_ref, rsin_ref, acos_ref, asin_ref, qg_ref, kg_ref,
                   rq_ref, rk_ref, rv_ref, rg_ref, q_ref, qt_ref, kt_ref, vt_ref):
    _project_tile(x_ref, g_ref, ws_ref, wt_ref, rcos_ref, rsin_ref, acos_ref, asin_ref, qg_ref, kg_ref,
                  rq_ref, rk_ref, rv_ref, rg_ref, qt_ref, kt_ref, vt_ref, q_ref=q_ref)


def _inproj_retention_kernel(x_ref, g_ref, ws_ref, wt_ref, rcos_ref, rsin_ref, acos_ref, asin_ref, qg_ref, kg_ref,
                             decay_ref, read_ref, write_ref, gl_ref, ng_ref,
                             y_ref, so_ref, qt_ref, kt_ref, vt_ref, kb_ref, vtb_ref,
                             rq_ref, rk_ref, rv_ref, rg_ref, st_ref, *, tiles_per_seq):
    t = pl.program_id(0) % tiles_per_seq

    @pl.when(t == 0)
    def _():
        st_ref[...] = jnp.zeros(st_ref.shape, F32)

    _project_tile(x_ref, g_ref, ws_ref, wt_ref, rcos_ref, rsin_ref, acos_ref, asin_ref, qg_ref, kg_ref,
                  rq_ref, rk_ref, rv_ref, rg_ref, qt_ref, kt_ref, vt_ref, kb_ref=kb_ref, vtb_ref=vtb_ref)
    _retention_chunks(rq_ref, rk_ref, rv_ref, rg_ref, decay_ref, read_ref, write_ref, gl_ref, ng_ref, y_ref, st_ref)

    @pl.when(t == tiles_per_seq - 1)
    def _():
        so_ref[...] = st_ref[...]


def _inproj_specs(d, ws, wt, tm, nt):
    row = lambda i: (i, 0)
    return [
        pl.BlockSpec((tm, d), row),
        _const_spec((1, d)),
        _const_spec(ws.shape),
        _const_spec(wt.shape),
        pl.BlockSpec((tm, RET_DK), lambda i: (i % nt, 0)),
        pl.BlockSpec((tm, RET_DK), lambda i: (i % nt, 0)),
        pl.BlockSpec((ATT_HD // 2, tm), lambda i: (0, i % nt)),
        pl.BlockSpec((ATT_HD // 2, tm), lambda i: (0, i % nt)),
        _const_spec((ATT_HD, 1)),
        _const_spec((ATT_HD, 1)),
    ]


def _inproj(x, g, ws, wt, tabs, qg, kg, nb, s_len, tm):
    m, d = x.shape
    nt = s_len // tm
    row_out = pl.BlockSpec((tm, D_RET), lambda i: (i, 0))
    t_out = pl.BlockSpec((None, ATT_HEADS, ATT_HD, tm), lambda i: (i // nt, 0, 0, i % nt))
    t_shape = jax.ShapeDtypeStruct((nb, ATT_HEADS, ATT_HD, s_len), F32)
    r_shape = jax.ShapeDtypeStruct((m, D_RET), F32)
    return pl.pallas_call(
        _inproj_kernel,
        grid=(m // tm,),
        in_specs=_inproj_specs(d, ws, wt, tm, nt),
        out_specs=[row_out, row_out, row_out, row_out, row_out, t_out, t_out, t_out],
        out_shape=[r_shape, r_shape, r_shape, r_shape, r_shape, t_shape, t_shape, t_shape],
        compiler_params=_params("arbitrary"),
        name="inproj",
    )(x, g, ws, wt, *tabs, qg, kg)


def _inproj_retention(x, g, ws, wt, tabs, qg, kg, rtabs, ng, nb, s_len, tm):
    m, d = x.shape
    nt = s_len // tm
    decay, read, write, gl = rtabs
    row_out = pl.BlockSpec((tm, D_RET), lambda i: (i, 0))
    t_out = pl.BlockSpec((None, ATT_HEADS, ATT_HD, tm), lambda i: (i // nt, 0, 0, i % nt))
    st_out = pl.BlockSpec((None, RET_HEADS, RET_DK, RET_DV), lambda i: (i // nt, 0, 0, 0))
    t_shape = jax.ShapeDtypeStruct((nb, ATT_HEADS, ATT_HD, s_len), F32)
    tile = pltpu.VMEM((tm, D_RET), F32)
    return pl.pallas_call(
        functools.partial(_inproj_retention_kernel, tiles_per_seq=nt),
        grid=(m // tm,),
        in_specs=_inproj_specs(d, ws, wt, tm, nt) + [_const_spec(t.shape) for t in (decay, read, write, gl, ng)],
        out_specs=[row_out, st_out, t_out, t_out, t_out, row_out, t_out],
        out_shape=[jax.ShapeDtypeStruct((m, D_RET), F32),
                   jax.ShapeDtypeStruct((nb, RET_HEADS, RET_DK, RET_DV), F32),
                   t_shape, t_shape, t_shape,
                   jax.ShapeDtypeStruct((m, D_ATT), BF16), jax.ShapeDtypeStruct(t_shape.shape, BF16)],
        scratch_shapes=[tile, tile, tile, tile, pltpu.VMEM((RET_HEADS, RET_DK, RET_DV), F32)],
        compiler_params=_params("arbitrary"),
        name="inproj_retention",
    )(x, g, ws, wt, *tabs, qg, kg, decay, read, write, gl, ng)


def _retention_kernel(rq_ref, rk_ref, rv_ref, rg_ref, s0_ref, decay_ref, read_ref, write_ref, gl_ref, ng_ref,
                      y_ref, so_ref, st_ref):
    c = pl.program_id(1)

    @pl.when(c == 0)
    def _():
        st_ref[...] = s0_ref[...]

    _retention_chunks(rq_ref, rk_ref, rv_ref, rg_ref, decay_ref, read_ref, write_ref, gl_ref, ng_ref, y_ref, st_ref)

    @pl.when(c == pl.num_programs(1) - 1)
    def _():
        so_ref[...] = st_ref[...]


def _retention(rq, rk, rv, rg, state0, tabs, ng, nb, n_chunks, chunk):
    m = rq.shape[0]
    decay, read, write, gl = tabs
    per_step = next(c for c in (4, 2, 1) if n_chunks % c == 0)
    n_steps = n_chunks // per_step
    row = pl.BlockSpec((per_step * chunk, D_RET), lambda b, c: (b * n_steps + c, 0))
    st = pl.BlockSpec((None, RET_HEADS, RET_DK, RET_DV), lambda b, c: (b, 0, 0, 0))
    cs = lambda shape: pl.BlockSpec(shape, lambda b, c: (0,) * len(shape))
    return pl.pallas_call(
        _retention_kernel,
        grid=(nb, n_steps),
        in_specs=[row, row, row, row, st, cs(decay.shape), cs(read.shape), cs(write.shape), cs(gl.shape), cs(ng.shape)],
        out_specs=[row, st],
        out_shape=[jax.ShapeDtypeStruct((m, D_RET), F32), jax.ShapeDtypeStruct(state0.shape, F32)],
        scratch_shapes=[pltpu.VMEM((RET_HEADS, RET_DK, RET_DV), F32)],
        compiler_params=_params("arbitrary", "arbitrary"),
        name="retention",
    )(rq, rk, rv, rg, state0, decay, read, write, gl, ng)


def _select_blocks(gate, n_past):
    nb = gate.shape[0]
    blk = lax.broadcasted_iota(jnp.int32, gate.shape, 0)
    valid = blk < n_past
    gm = jnp.where(valid, gate, NEG_INF)
    rank = jnp.zeros(gate.shape, jnp.int32)
    for m in range(nb):
        row = gm[m:m + 1, :]
        ahead = (row > gm) | ((row == gm) & (blk > m))
        rank = rank + ahead.astype(jnp.int32)
    return valid & (rank < MOBA_TOPK) & (jnp.abs(gate) < jnp.inf)


def _moba_prompt_kernel(qt_ref, kt_ref, kb_ref, vt_ref, o_ref, km_ref, bias_ref, s_ref, acc_ref):
    qi = pl.program_id(1)
    nb = kt_ref.shape[-1] // MOBA_BLOCK
    n_pairs = ATT_HEADS // 2

    @pl.when(qi == 0)
    def _():
        for h in range(ATT_HEADS):
            for n in range(nb):
                kblk = kt_ref[h, :, n * MOBA_BLOCK:(n + 1) * MOBA_BLOCK]
                mean = jnp.sum(kblk, axis=-1, keepdims=True) * (1.0 / MOBA_BLOCK)
                km_ref[h, n] = jnp.broadcast_to(mean, (ATT_HD, Q_BLOCK))

    ob = (qi * Q_BLOCK) // MOBA_BLOCK
    q_off = qi * Q_BLOCK - ob * MOBA_BLOCK

    q_pairs = []
    zero = jnp.zeros((ATT_HD, Q_BLOCK), BF16)
    for h in range(ATT_HEADS):
        qt = qt_ref[h]
        gate = jnp.concatenate(
            [jnp.sum(qt * km_ref[h, n], axis=0, keepdims=True) for n in range(nb)], axis=0)
        bias_ref[h] = jnp.where(_select_blocks(gate, ob), 0.0, NEG_INF).astype(F32)
        qs = (qt * ATT_HD ** -0.5).astype(BF16)
        if h % 2 == 0:
            top = jnp.concatenate([qs, zero], axis=1)
        else:
            q_pairs.append(jnp.concatenate([top, jnp.concatenate([zero, qs], axis=1)], axis=0))

    def scores(n, pair):
        start = pl.multiple_of(n * MOBA_BLOCK, MOBA_BLOCK)
        kb = kb_ref[pl.ds(start, MOBA_BLOCK), pair * 2 * ATT_HD:(pair + 1) * 2 * ATT_HD]
        return jnp.dot(kb, q_pairs[pair], preferred_element_type=F32)

    def values(n, h):
        start = pl.multiple_of(n * MOBA_BLOCK, MOBA_BLOCK)
        return vt_ref[h, :, pl.ds(start, MOBA_BLOCK)]

    key = lax.broadcasted_iota(jnp.int32, (MOBA_BLOCK, Q_BLOCK), 0)
    qry = lax.broadcasted_iota(jnp.int32, (MOBA_BLOCK, Q_BLOCK), 1)
    causal = jnp.where(key <= qry + q_off, 0.0, NEG_INF).astype(F32)

    def score_pass(n, masks, m_prev):
        m_new = []
        pair_scores = [scores(n, pair) for pair in range(n_pairs)]
        for pair in range(n_pairs):
            st = pair_scores[pair]
            for j in range(2):
                h = 2 * pair + j
                tile = st[:, j * Q_BLOCK:(j + 1) * Q_BLOCK] + masks[h]
                s_ref[n, h] = tile
                m_new.append(jnp.maximum(m_prev[h], jnp.max(tile, axis=0, keepdims=True)))
        return m_new

    m_own = score_pass(ob, [causal] * ATT_HEADS, [jnp.full((1, Q_BLOCK), NEG_INF, F32)] * ATT_HEADS)

    def past_scores(n, m_prev):
        return tuple(score_pass(n, [bias_ref[h, pl.ds(n, 1), :] for h in range(ATT_HEADS)], list(m_prev)))

    m_fin = lax.fori_loop(0, ob, past_scores, tuple(m_own))

    ones_rows = jnp.ones((SUM_ROWS, MOBA_BLOCK), BF16)

    def value_pass(n, acc_prev):
        ps = [jnp.exp((s_ref[n, h] - m_fin[h]).astype(BF16)) for h in range(ATT_HEADS)]
        return [acc_prev[h] + jnp.dot(jnp.concatenate([values(n, h), ones_rows], axis=0), ps[h],
                                      preferred_element_type=F32) for h in range(ATT_HEADS)]

    acc_own = value_pass(ob, [jnp.zeros((ATT_HD + SUM_ROWS, Q_BLOCK), F32)] * ATT_HEADS)
    for h in range(ATT_HEADS):
        acc_ref[h] = acc_own[h]

    def past_values(n, _):
        acc_new = value_pass(n, [acc_ref[h] for h in range(ATT_HEADS)])
        for h in range(ATT_HEADS):
            acc_ref[h] = acc_new[h]
        return 0

    lax.fori_loop(0, ob, past_values, 0)
    out_t = jnp.concatenate([acc_ref[h, 0:ATT_HD, :] / acc_ref[h, ATT_HD:ATT_HD + 1, :] for h in range(ATT_HEADS)],
                            axis=0)
    o_ref[...] = out_t.T


def _moba_prompt(qt, kt, kb, vtb, m_rows):
    nbat, _, _, s_len = kt.shape
    nqb = s_len // Q_BLOCK
    nblk = s_len // MOBA_BLOCK
    full = pl.BlockSpec((None, ATT_HEADS, ATT_HD, s_len), lambda b, i: (b, 0, 0, 0))
    return pl.pallas_call(
        _moba_prompt_kernel,
        grid=(nbat, nqb),
        in_specs=[
            pl.BlockSpec((None, ATT_HEADS, ATT_HD, Q_BLOCK), lambda b, i: (b, 0, 0, i)),
            full,
            pl.BlockSpec((s_len, D_ATT), lambda b, i: (b, 0)),
            full,
        ],
        out_specs=pl.BlockSpec((Q_BLOCK, D_ATT), lambda b, i: (b * nqb + i, 0)),
        out_shape=jax.ShapeDtypeStruct((m_rows, D_ATT), F32),
        scratch_shapes=[pltpu.VMEM((ATT_HEADS, nblk, ATT_HD, Q_BLOCK), F32),
                        pltpu.VMEM((ATT_HEADS, nblk, Q_BLOCK), F32),
                        pltpu.VMEM((nblk, ATT_HEADS, MOBA_BLOCK, Q_BLOCK), F32),
                        pltpu.VMEM((ATT_HEADS, ATT_HD + SUM_ROWS, Q_BLOCK), F32)],
        compiler_params=_params("arbitrary", "arbitrary"),
        name="moba_prompt",
    )(qt, kt, kb, vtb)


def _kmean_kernel(pt_ref, *refs):
    del pt_ref
    page_refs, o_ref = refs[:-1], refs[-1]
    j = pl.program_id(2)
    n_per_step = len(page_refs) // PAGES_PER_BLOCK

    @pl.when(j == 0)
    def _():
        o_ref[...] = jnp.zeros(o_ref.shape, F32)

    lane = lax.broadcasted_iota(jnp.int32, o_ref.shape, 1)
    out = o_ref[...]
    for p in range(n_per_step):
        tot = page_refs[PAGES_PER_BLOCK * p][...]
        for r in range(1, PAGES_PER_BLOCK):
            tot = tot + page_refs[PAGES_PER_BLOCK * p + r][...]
        mean = jnp.sum(tot.reshape(D_ATT, PAGE_SIZE), axis=-1, keepdims=True) * (1.0 / MOBA_BLOCK)
        out = jnp.where(lane == j * n_per_step + p, mean, out)
    o_ref[...] = out


def _cache_kmean(cache_t, page_table, nbf, blocks_per_step):
    depth = cache_t.shape[0]
    db = page_table.shape[0]
    pages_per_step = blocks_per_step * PAGES_PER_BLOCK

    def page_spec(i):
        return pl.BlockSpec((None, None, ATT_HEADS, ATT_HD, PAGE_SIZE),
                            lambda l, b, j, pt: (l, pt[b, j * pages_per_step + i], 0, 0, 0))

    return pl.pallas_call(
        _kmean_kernel,
        grid_spec=pltpu.PrefetchScalarGridSpec(
            num_scalar_prefetch=1,
            grid=(depth, db, nbf // blocks_per_step),
            in_specs=[page_spec(i) for i in range(pages_per_step)],
            out_specs=pl.BlockSpec((None, None, D_ATT, nbf), lambda l, b, j, pt: (l, b, 0, 0)),
        ),
        out_shape=jax.ShapeDtypeStruct((depth, db, D_ATT, nbf), F32),
        compiler_params=_params("arbitrary", "arbitrary", "arbitrary"),
        name="cache_kmean",
    )(page_table, *([cache_t] * pages_per_step))


def _topk_kernel(q_ref, km_ref, idx_ref):
    nbf = km_ref.shape[-1]
    n_q = q_ref.shape[0]
    lane = lax.broadcasted_iota(jnp.int32, (n_q, nbf), 1).astype(F32)
    out_lane = lax.broadcasted_iota(jnp.int32, (n_q, 128), 1)
    for h in range(ATT_HEADS):
        q = q_ref[:, h * ATT_HD:(h + 1) * ATT_HD]
        gate = jnp.dot(q, km_ref[h * ATT_HD:(h + 1) * ATT_HD, :], precision=lax.Precision.HIGHEST,
                       preferred_element_type=F32)
        out = jnp.zeros((n_q, 128), jnp.int32)
        for t in range(MOBA_TOPK):
            best = jnp.max(gate, axis=-1, keepdims=True)
            idx = jnp.min(jnp.where(gate == best, lane, float(nbf)), axis=-1, keepdims=True)
            out = jnp.where(out_lane == t, idx.astype(jnp.int32), out)
            gate = jnp.where(lane == idx, NEG_INF, gate)
        idx_ref[h] = out


def _sample_topk(q, kmean_l):
    db, _, nbf = kmean_l.shape
    n_q = q.shape[0] // db
    return pl.pallas_call(
        _topk_kernel,
        grid=(db,),
        in_specs=[pl.BlockSpec((n_q, D_ATT), lambda b: (b, 0)),
                  pl.BlockSpec((None, D_ATT, nbf), lambda b: (b, 0, 0))],
        out_specs=pl.BlockSpec((None, ATT_HEADS, n_q, 128), lambda b: (b, 0, 0, 0)),
        out_shape=jax.ShapeDtypeStruct((db, ATT_HEADS, n_q, 128), jnp.int32),
        compiler_params=_params("arbitrary"),
        name="sample_topk",
    )(q, kmean_l)


def _moba_sample_kernel(gidx_ref, pt_ref, qt_ref, ktn_ref, vtn_ref, ck_hbm, cv_hbm, o_ref, kbuf, vbuf, sem, *, layer):
    n_hsteps = pl.num_programs(1)
    step = pl.program_id(0) * n_hsteps + pl.program_id(1)
    n_steps = pl.num_programs(0) * n_hsteps
    heads_per_step, _, n_q = qt_ref.shape
    n_sel = MOBA_TOPK * PAGES_PER_BLOCK
    picks_per_head = n_q * MOBA_TOPK
    scale = ATT_HD ** -0.5

    def for_each_copy(step_, slot_, fn):
        b_ = step_ // n_hsteps
        h0 = (step_ % n_hsteps) * heads_per_step
        base = step_ * (heads_per_step * picks_per_head)

        for hh in range(heads_per_step):
            def body(g, _, hh=hh):
                blk = gidx_ref[base + hh * picks_per_head + g]
                for r in range(PAGES_PER_BLOCK):
                    pid = pt_ref[b_, blk * PAGES_PER_BLOCK + r]
                    i = (hh * picks_per_head + g) * PAGES_PER_BLOCK + r
                    fn(pltpu.make_async_copy(ck_hbm.at[layer, pid, h0 + hh], kbuf.at[slot_, i], sem.at[slot_, 0]))
                    fn(pltpu.make_async_copy(cv_hbm.at[layer, pid, h0 + hh], vbuf.at[slot_, i], sem.at[slot_, 1]))
                return 0

            lax.fori_loop(0, picks_per_head, body, 0, unroll=4)

    def start_all(step_, slot_):
        for_each_copy(step_, slot_, lambda c: c.start())

    def wait_all(step_, slot_):
        for_each_copy(step_, slot_, lambda c: c.wait())

    slot = step % 2

    @pl.when(step == 0)
    def _():
        start_all(step, slot)

    @pl.when(step + 1 < n_steps)
    def _():
        start_all(step + 1, 1 - slot)

    wait_all(step, slot)

    def rows(fn):
        return jnp.concatenate([fn(l) for l in range(n_q)], axis=0)

    qry = lax.broadcasted_iota(jnp.int32, (n_q, n_q), 0)
    key = lax.broadcasted_iota(jnp.int32, (n_q, n_q), 1)
    for hh in range(heads_per_step):
        qt = qt_ref[hh]
        ktn = ktn_ref[hh]
        vtn = vtn_ref[hh]
        page0 = hh * n_q * n_sel
        q_cols = [qt[:, l:l + 1] for l in range(n_q)]
        q_wide = [jnp.broadcast_to(qc, (ATT_HD, PAGE_SIZE)) for qc in q_cols]
        s_new = rows(lambda l: jnp.sum(q_cols[l] * ktn, axis=0, keepdims=True)) * scale
        s_new = jnp.where(key <= qry, s_new, NEG_INF)
        s_sel = [rows(lambda l: jnp.sum(q_wide[l] * kbuf[slot, page0 + l * n_sel + j], axis=0, keepdims=True)) * scale
                 for j in range(n_sel)]
        m = jnp.max(s_new, axis=-1, keepdims=True)
        for s in s_sel:
            m = jnp.maximum(m, jnp.max(s, axis=-1, keepdims=True))
        p_new = jnp.exp(s_new - m)
        p_sel = [jnp.exp(s - m) for s in s_sel]
        denom = jnp.sum(p_new, axis=-1, keepdims=True)
        for p in p_sel:
            denom = denom + jnp.sum(p, axis=-1, keepdims=True)
        for l in range(n_q):
            acc = p_sel[0][l:l + 1, :] * vbuf[slot, page0 + l * n_sel]
            for j in range(1, n_sel):
                acc = acc + p_sel[j][l:l + 1, :] * vbuf[slot, page0 + l * n_sel + j]
            out = jnp.sum(acc, axis=-1, keepdims=True) + jnp.sum(p_new[l:l + 1, :] * vtn, axis=-1, keepdims=True)
            o_ref[hh, :, l:l + 1] = out / denom[l:l + 1, :]


def _moba_sample(gidx, page_table, qt, ktn, vtn, cache_k_t, cache_v_t, layer):
    db, _, _, n_q = qt.shape
    heads_per_step = 2
    n_pages = heads_per_step * n_q * MOBA_TOPK * PAGES_PER_BLOCK
    small = pl.BlockSpec((None, heads_per_step, ATT_HD, n_q), lambda b, h, gi, pt: (b, h, 0, 0))
    hbm = pl.BlockSpec(memory_space=pl.ANY)
    return pl.pallas_call(
        functools.partial(_moba_sample_kernel, layer=layer),
        grid_spec=pltpu.PrefetchScalarGridSpec(
            num_scalar_prefetch=2,
            grid=(db, ATT_HEADS // heads_per_step),
            in_specs=[small, small, small, hbm, hbm],
            out_specs=small,
            scratch_shapes=[pltpu.VMEM((2, n_pages, ATT_HD, PAGE_SIZE), F32),
                            pltpu.VMEM((2, n_pages, ATT_HD, PAGE_SIZE), F32),
                            pltpu.SemaphoreType.DMA((2, 2))],
        ),
        out_shape=jax.ShapeDtypeStruct(qt.shape, F32),
        compiler_params=_params("arbitrary", "arbitrary"),
        name="moba_sample",
    )(gidx, page_table, qt, ktn, vtn, cache_k_t, cache_v_t)


FF_CHUNK = 256
KMEAN_SLOTS = 3


def _mix_ffn_body(yr_ref, ao_ref, h_ref, wo_ref, g_ref, wg_ref, wu_ref, wd_ref, o_ref, before_chunk=None):
    h1 = (h_ref[...]
          + jnp.dot(yr_ref[...].astype(BF16), wo_ref[0:D_RET, :], preferred_element_type=F32)
          + jnp.dot(ao_ref[...].astype(BF16), wo_ref[D_RET:D_RET + D_ATT, :], preferred_element_type=F32))
    hn = (h1 * lax.rsqrt(jnp.mean(h1 * h1, axis=-1, keepdims=True) + RMS_EPS) * g_ref[...]).astype(BF16)
    acc = jnp.zeros_like(h1)
    d_ff = wg_ref.shape[1]
    for c in range(d_ff // FF_CHUNK):
        if before_chunk is not None:
            before_chunk(c)
        sl = slice(c * FF_CHUNK, (c + 1) * FF_CHUNK)
        gate = jnp.dot(hn, wg_ref[:, sl], preferred_element_type=F32)
        up = jnp.dot(hn, wu_ref[:, sl], preferred_element_type=F32)
        act = (gate * jax.nn.sigmoid(gate) * up).astype(BF16)
        acc = acc + jnp.dot(act, wd_ref[sl, :], preferred_element_type=F32)
    o_ref[...] = h1 + acc


def _mix_ffn_kernel(yr_ref, ao_ref, h_ref, wo_ref, g_ref, wg_ref, wu_ref, wd_ref, o_ref):
    _mix_ffn_body(yr_ref, ao_ref, h_ref, wo_ref, g_ref, wg_ref, wu_ref, wd_ref, o_ref)


def _mix_ffn_kmean_kernel(pt_ref, yr_ref, ao_ref, h_ref, wo_ref, g_ref, wg_ref, wu_ref, wd_ref, ck_hbm,
                          o_ref, km_ref, pbuf, sem, *, layer, groups_per_tile, chunk_stride):
    i = pl.program_id(0)
    n_groups = pl.num_programs(0) * groups_per_tile
    n_pages = pt_ref.shape[1]
    pages_per_group = pbuf.shape[1]
    blocks_per_group = pages_per_group // PAGES_PER_BLOCK
    tiles_per_seq = n_pages // (groups_per_tile * pages_per_group)
    t = i % tiles_per_seq
    ahead = KMEAN_SLOTS - 1

    def for_each_copy(group, fn):
        page0 = group * pages_per_group
        b_ = page0 // n_pages
        p0 = page0 % n_pages
        slot_ = group % KMEAN_SLOTS
        for r in range(pages_per_group):
            fn(pltpu.make_async_copy(ck_hbm.at[layer, pt_ref[b_, p0 + r]], pbuf.at[slot_, r], sem.at[slot_]))

    @pl.when(i == 0)
    def _():
        for g in range(ahead):
            @pl.when(g < n_groups)
            def _():
                for_each_copy(g, lambda c: c.start())

    @pl.when(t == 0)
    def _():
        km_ref[...] = jnp.zeros(km_ref.shape, F32)

    lane = lax.broadcasted_iota(jnp.int32, km_ref.shape, 1)

    def before_chunk(c):
        if c % chunk_stride != 0 or c // chunk_stride >= groups_per_tile:
            return
        q = c // chunk_stride
        group = i * groups_per_tile + q

        @pl.when(group + ahead < n_groups)
        def _():
            for_each_copy(group + ahead, lambda cp: cp.start())

        for_each_copy(group, lambda cp: cp.wait())
        slot = group % KMEAN_SLOTS
        out = km_ref[...]
        for p in range(blocks_per_group):
            tot = pbuf[slot, PAGES_PER_BLOCK * p]
            for r in range(1, PAGES_PER_BLOCK):
                tot = tot + pbuf[slot, PAGES_PER_BLOCK * p + r]
            mean = jnp.sum(tot.reshape(D_ATT, PAGE_SIZE), axis=-1, keepdims=True) * (1.0 / MOBA_BLOCK)
            out = jnp.where(lane == (t * groups_per_tile + q) * blocks_per_group + p, mean, out)
        km_ref[...] = out

    _mix_ffn_body(yr_ref, ao_ref, h_ref, wo_ref, g_ref, wg_ref, wu_ref, wd_ref, o_ref, before_chunk)


def _mix_ffn(yr, ao, h, wo, g, wg, wu, wd, tm, kmean_stream=None):
    m, d = h.shape
    row = lambda w: pl.BlockSpec((tm, w), lambda i, *_: (i, 0))
    in_specs = [row(D_RET), row(D_ATT), row(d), _const_spec(wo.shape), _const_spec((1, d)),
                _const_spec(wg.shape), _const_spec(wu.shape), _const_spec(wd.shape)]
    out_shape = jax.ShapeDtypeStruct((m, d), F32)
    if kmean_stream is None:
        return pl.pallas_call(
            _mix_ffn_kernel, grid=(m // tm,), in_specs=in_specs, out_specs=row(d), out_shape=out_shape,
            compiler_params=_params("arbitrary"), name="mix_ffn",
        )(yr, ao, h, wo, g, wg, wu, wd)
    cache_k_t, page_table, layer, groups_per_tile = kmean_stream
    db, n_pages = page_table.shape
    n_tiles = m // tm
    pages_per_group = db * n_pages // (n_tiles * groups_per_tile)
    tiles_per_seq = n_tiles // db
    nbf = n_pages // PAGES_PER_BLOCK
    n_chunks = wg.shape[1] // FF_CHUNK
    return pl.pallas_call(
        functools.partial(_mix_ffn_kmean_kernel, layer=layer, groups_per_tile=groups_per_tile,
                          chunk_stride=n_chunks // groups_per_tile),
        grid_spec=pltpu.PrefetchScalarGridSpec(
            num_scalar_prefetch=1,
            grid=(n_tiles,),
            in_specs=in_specs + [pl.BlockSpec(memory_space=pl.ANY)],
            out_specs=[row(d), pl.BlockSpec((None, D_ATT, nbf), lambda i, pt: (i // tiles_per_seq, 0, 0))],
            scratch_shapes=[pltpu.VMEM((KMEAN_SLOTS, pages_per_group, ATT_HEADS, ATT_HD, PAGE_SIZE), F32),
                            pltpu.SemaphoreType.DMA((KMEAN_SLOTS,))],
        ),
        out_shape=[out_shape, jax.ShapeDtypeStruct((db, D_ATT, nbf), F32)],
        compiler_params=_params("arbitrary"),
        name="mix_ffn_kmean",
    )(page_table, yr, ao, h, wo, g, wg, wu, wd, cache_k_t)


def _rope_tables(pos):
    ret_freq = 1.0 / (ROPE_THETA ** jnp.linspace(0.0, 1.0, RET_DK // 2, dtype=F32))
    att_freq = 1.0 / (ROPE_THETA ** (jnp.arange(0, ATT_HD, 2, dtype=F32) / ATT_HD))
    ang_r = pos.astype(F32)[:, None] * ret_freq[None, :]
    ang_a = pos.astype(F32)[:, None] * att_freq[None, :]
    rcos = jnp.concatenate([jnp.cos(ang_r), jnp.cos(ang_r)], axis=-1)
    rsin = jnp.concatenate([-jnp.sin(ang_r), jnp.sin(ang_r)], axis=-1)
    return rcos, rsin, jnp.cos(ang_a).T, jnp.sin(ang_a).T


def _retention_tables(chunk):
    log_gamma = jnp.log1p(-jnp.exp2(-5.0 - jnp.arange(RET_HEADS, dtype=F32)))
    i = jnp.arange(chunk, dtype=F32)
    diff = i[:, None] - i[None, :]
    decay = jnp.where(diff >= 0, jnp.exp(log_gamma[:, None, None] * jnp.maximum(diff, 0.0)), 0.0)
    read = jnp.exp(log_gamma[None, :] * (i[:, None] + 1.0))
    write = jnp.exp(log_gamma[None, :] * (chunk - 1.0 - i[:, None]))
    gl = jnp.exp(log_gamma * chunk)[None, :]
    wide = lambda t: jnp.repeat(t, RET_DV, axis=1)
    return decay, wide(read), wide(write), wide(gl)


def kernel(x_prompt, x_sample, cache_k, cache_v, state_ret, page_table, norm1_g, w_in, q_norm_g, k_norm_g,
           ret_norm_g, w_out, norm2_g, w_gate, w_up, w_down):
    nbat, s_len, d_model = x_prompt.shape
    db, n_q, _ = x_sample.shape
    depth = w_in.shape[0]
    n_pages = page_table.shape[1]
    past = n_pages * PAGE_SIZE
    nbf = past // MOBA_BLOCK
    assert n_pages % PAGES_PER_BLOCK == 0, "the new tokens must start a fresh MoBA block"
    assert nbf >= MOBA_TOPK and s_len % MOBA_BLOCK == 0 and s_len % RET_CHUNK == 0

    cache_k_t = jnp.transpose(cache_k, (0, 1, 3, 4, 2))
    cache_v_t = jnp.transpose(cache_v, (0, 1, 3, 4, 2))

    tabs_p = _rope_tables(jnp.arange(s_len))
    rc, rs, ac, as_ = _rope_tables(past + jnp.arange(n_q))
    tabs_s = (jnp.tile(rc, (db, 1)), jnp.tile(rs, (db, 1)), jnp.tile(ac, (1, db)), jnp.tile(as_, (1, db)))
    rtab_p = _retention_tables(RET_CHUNK)
    rtab_s = _retention_tables(n_q)

    hp = x_prompt.reshape(nbat * s_len, d_model)
    hs = x_sample.reshape(db * n_q, d_model)
    tm_p = 512 if s_len % 512 == 0 else RET_CHUNK

    n_tiles = nbat * s_len // tm_p
    pages_per_tile = db * n_pages // n_tiles if (db * n_pages) % n_tiles == 0 else 0
    groups_per_tile = next((g for g in (4, 2, 1) if pages_per_tile and pages_per_tile % (g * PAGES_PER_BLOCK) == 0), 0)
    stream_kmean = groups_per_tile > 0 and n_tiles % db == 0 and n_pages % pages_per_tile == 0
    if not stream_kmean:
        kmean = _cache_kmean(cache_k_t, page_table, nbf, 8 if nbf % 8 == 0 else 1)
    kp, vp, rp, kss, vss, rss = [], [], [], [], [], []
    d_split = 2 * RET_HEADS * RET_DK + 2 * D_RET
    for l in range(depth):
        ws = w_in[l, :, :d_split].astype(BF16)
        wt = w_in[l, :, d_split:].T.astype(BF16)
        g1 = norm1_g[l][None, :]
        g2 = norm2_g[l][None, :]
        qg = q_norm_g[l][:, None]
        kg = k_norm_g[l][:, None]
        ng = ret_norm_g[l].reshape(1, D_RET)
        wo = w_out[l].astype(BF16)
        wg = w_gate[l].astype(BF16)
        wu = w_up[l].astype(BF16)
        wd = w_down[l].astype(BF16)

        yr, rst, aqt, akt, avt, akb, avtb = _inproj_retention(hp, g1, ws, wt, tabs_p, qg, kg, rtab_p, ng, nbat, s_len, tm_p)
        ao = _moba_prompt(aqt, akt, akb, avtb, nbat * s_len)
        if stream_kmean:
            hp, kmean_l = _mix_ffn(yr, ao, hp, wo, g2, wg, wu, wd, tm_p, (cache_k_t, page_table, l, groups_per_tile))
        else:
            hp, kmean_l = _mix_ffn(yr, ao, hp, wo, g2, wg, wu, wd, tm_p), kmean[l]
        kp.append(akt)
        vp.append(avt)
        rp.append(rst)

        m_s = db * n_q
        rq, rk, rv, rg, aq, aqt, akt, avt = _inproj(hs, g1, ws, wt, tabs_s, qg, kg, 1, m_s, m_s)
        yr, rst = _retention(rq, rk, rv, rg, state_ret[l], rtab_s, ng, db, 1, n_q)
        gidx = _sample_topk(aq, kmean_l)[..., :MOBA_TOPK].reshape(-1)
        per_seq = lambda t: t.reshape(ATT_HEADS, ATT_HD, db, n_q).transpose(2, 0, 1, 3)
        aot = _moba_sample(gidx, page_table, per_seq(aqt), per_seq(akt), per_seq(avt), cache_k_t, cache_v_t, l)
        ao = aot.transpose(0, 3, 1, 2).reshape(m_s, D_ATT)
        hs = _mix_ffn(yr, ao, hs, wo, g2, wg, wu, wd, m_s)
        rows = lambda t: t.reshape(D_ATT, db, n_q).transpose(1, 2, 0).reshape(db, n_q, ATT_HEADS, ATT_HD)
        kss.append(rows(akt))
        vss.append(rows(avt))
        rss.append(rst)

    seq_major = lambda ts: jnp.transpose(jnp.stack(ts), (0, 1, 4, 2, 3))
    return (hp.reshape(nbat, s_len, d_model), hs.reshape(db, n_q, d_model), seq_major(kp), seq_major(vp),
            jnp.stack(rp), jnp.stack(kss), jnp.stack(vss), jnp.stack(rss))
```

```python
import functools

import jax
import jax.numpy as jnp
from jax import lax
from jax.experimental import pallas as pl
from jax.experimental.pallas import tpu as pltpu

F32 = jnp.float32
BF16 = jnp.bfloat16

PAGE_SIZE = 128
RET_HEADS = 4
RET_DK = 128
RET_DV = 128
RET_CHUNK = 128
ATT_HEADS = 8
ATT_HD = 64
MOBA_BLOCK = 256
MOBA_TOPK = 3
Q_BLOCK = 256
ROPE_THETA = 10000.0
RMS_EPS = 1e-6
D_RET = RET_HEADS * RET_DV
D_ATT = ATT_HEADS * ATT_HD
PAGES_PER_BLOCK = MOBA_BLOCK // PAGE_SIZE
SUM_ROWS = 16

VMEM_LIMIT_BYTES = 56 * 1024 * 1024
NEG_INF = float("-inf")

_NT = (((1,), (1,)), ((), ()))
_TN = (((0,), (0,)), ((), ()))


def _params(*sem):
    return pltpu.CompilerParams(dimension_semantics=sem, vmem_limit_bytes=VMEM_LIMIT_BYTES)


def _const_spec(shape):
    nd = len(shape)
    return pl.BlockSpec(shape, lambda *_: (0,) * nd, pipeline_mode=pl.Buffered(1))


def _project_tile(x_ref, g_ref, ws_ref, wt_ref, rcos_ref, rsin_ref, acos_ref, asin_ref, qg_ref, kg_ref,
                  rq_ref, rk_ref, rv_ref, rg_ref, qt_ref, kt_ref, vt_ref, q_ref=None, kb_ref=None, vtb_ref=None):
    x = x_ref[...]
    xn = (x * lax.rsqrt(jnp.mean(x * x, axis=-1, keepdims=True) + RMS_EPS) * g_ref[...]).astype(BF16)
    tm = x.shape[0]

    def std(c):
        return jnp.dot(xn, ws_ref[:, c * D_RET:(c + 1) * D_RET], preferred_element_type=F32)

    rcos = rcos_ref[...]
    rsin = rsin_ref[...]
    for c, o_ref, scale in ((0, rq_ref, None), (1, rk_ref, RET_DK ** -0.5)):
        acc = std(c)
        for h in range(RET_HEADS):
            xh = acc[:, h * RET_DK:(h + 1) * RET_DK]
            r = xh * rcos + pltpu.roll(xh, RET_DK // 2, 1) * rsin
            o_ref[:, h * RET_DK:(h + 1) * RET_DK] = r if scale is None else r * scale
    rv_ref[...] = std(2)
    rg_ref[...] = std(3)

    def tr(c):
        return lax.dot_general(wt_ref[c * D_ATT:(c + 1) * D_ATT, :], xn, _NT, preferred_element_type=F32)

    acos = acos_ref[...]
    asin = asin_ref[...]
    half = ATT_HD // 2
    for c, gn_ref, o_ref in ((0, qg_ref, qt_ref), (1, kg_ref, kt_ref)):
        acc = tr(c)
        for h in range(ATT_HEADS):
            xh = acc[h * ATT_HD:(h + 1) * ATT_HD, :]
            y = xh * lax.rsqrt(jnp.mean(xh * xh, axis=0, keepdims=True) + RMS_EPS) * gn_ref[...]
            y1 = y[:half]
            y2 = y[half:]
            o_ref[h, 0:half, :] = y1 * acos - y2 * asin
            o_ref[h, half:ATT_HD, :] = y2 * acos + y1 * asin
    if q_ref is not None:
        q_ref[...] = qt_ref[...].reshape(D_ATT, tm).T
    if kb_ref is not None:
        kb_ref[...] = kt_ref[...].reshape(D_ATT, tm).T.astype(BF16)
    vt = tr(2).reshape(ATT_HEADS, ATT_HD, tm)
    vt_ref[...] = vt
    if vtb_ref is not None:
        vtb_ref[...] = vt.astype(BF16)


def _retention_chunks(rq_ref, rk_ref, rv_ref, rg_ref, decay_ref, read_ref, write_ref, gl_ref, ng_ref, y_ref, st_ref):
    chunk = decay_ref.shape[-1]
    sls = [slice(h * RET_DK, (h + 1) * RET_DK) for h in range(RET_HEADS)]
    states = [st_ref[h] for h in range(RET_HEADS)]
    for ci in range(rq_ref.shape[0] // chunk):
        rows = slice(ci * chunk, (ci + 1) * chunk)
        qs = [rq_ref[rows, sl].astype(BF16) for sl in sls]
        ks = [rk_ref[rows, sl] for sl in sls]
        vs = [rv_ref[rows, sl].astype(BF16) for sl in sls]
        scores = [lax.dot_general(qs[h], ks[h].astype(BF16), _NT, preferred_element_type=F32) for h in range(RET_HEADS)]
        cross = [jnp.dot(qs[h], states[h].astype(BF16), preferred_element_type=F32) for h in range(RET_HEADS)]
        for h in range(RET_HEADS):
            kw = (ks[h] * write_ref[:, sls[h]]).astype(BF16)
            states[h] = gl_ref[:, sls[h]] * states[h] + lax.dot_general(kw, vs[h], _TN, preferred_element_type=F32)
        for h in range(RET_HEADS):
            s = (scores[h] * decay_ref[h]).astype(BF16)
            o = jnp.dot(s, vs[h], preferred_element_type=F32) + cross[h] * read_ref[:, sls[h]]
            y = o * lax.rsqrt(jnp.mean(o * o, axis=-1, keepdims=True) + RMS_EPS) * ng_ref[:, sls[h]]
            g = rg_ref[rows, sls[h]]
            y_ref[rows, sls[h]] = y * (g * jax.nn.sigmoid(g))
    for h in range(RET_HEADS):
        st_ref[h] = states[h]


def _inproj_kernel(x_ref, g_ref, ws_ref, wt_ref, rcos_ref, rsin_ref, acos_ref, asin_ref, qg_ref, kg_ref,
                   rq_ref, rk_ref, rv_ref, rg_ref, q_ref, qt_ref, kt_ref, vt_ref):
    _project_tile(x_ref, g_ref, ws_ref, wt_ref, rcos_ref, rsin_ref, acos_ref, asin_ref, qg_ref, kg_ref,
                  rq_ref, rk_ref, rv_ref, rg_ref, qt_ref, kt_ref, vt_ref, q_ref=q_ref)


def _inproj_retention_kernel(*refs, tiles_per_seq, first_layer):
    (x_ref, g_ref, ws_ref, wt_ref, rcos_ref, rsin_ref, acos_ref, asin_ref, qg_ref, kg_ref,
     decay_ref, read_ref, write_ref, gl_ref, ng_ref) = refs[:15]
    (y_ref, so_ref, qt_ref, kt_ref, vt_ref, kb_ref, vtb_ref,
     rq_ref, rk_ref, rv_ref, rg_ref, st_ref) = refs[-12:]
    t = pl.program_id(0) % tiles_per_seq

    @pl.when(t == 0)
    def _():
        st_ref[...] = jnp.zeros(st_ref.shape, F32)

    if first_layer:
        for l in range(1, kt_ref.shape[0]):
            kt_ref[l] = jnp.zeros(kt_ref.shape[1:], F32)
            vt_ref[l] = jnp.zeros(vt_ref.shape[1:], F32)
        kt_ref = kt_ref.at[0]
        vt_ref = vt_ref.at[0]
    _project_tile(x_ref, g_ref, ws_ref, wt_ref, rcos_ref, rsin_ref, acos_ref, asin_ref, qg_ref, kg_ref,
                  rq_ref, rk_ref, rv_ref, rg_ref, qt_ref, kt_ref, vt_ref, kb_ref=kb_ref, vtb_ref=vtb_ref)
    _retention_chunks(rq_ref, rk_ref, rv_ref, rg_ref, decay_ref, read_ref, write_ref, gl_ref, ng_ref, y_ref, st_ref)

    @pl.when(t == tiles_per_seq - 1)
    def _():
        so_ref[...] = st_ref[...]


def _inproj_specs(d, ws, wt, tm, nt):
    row = lambda i: (i, 0)
    return [
        pl.BlockSpec((tm, d), row),
        _const_spec((1, d)),
        _const_spec(ws.shape),
        _const_spec(wt.shape),
        pl.BlockSpec((tm, RET_DK), lambda i: (i % nt, 0)),
        pl.BlockSpec((tm, RET_DK), lambda i: (i % nt, 0)),
        pl.BlockSpec((ATT_HD // 2, tm), lambda i: (0, i % nt)),
        pl.BlockSpec((ATT_HD // 2, tm), lambda i: (0, i % nt)),
        _const_spec((ATT_HD, 1)),
        _const_spec((ATT_HD, 1)),
    ]


def _inproj(x, g, ws, wt, tabs, qg, kg, nb, s_len, tm):
    m, d = x.shape
    nt = s_len // tm
    row_out = pl.BlockSpec((tm, D_RET), lambda i: (i, 0))
    t_out = pl.BlockSpec((None, ATT_HEADS, ATT_HD, tm), lambda i: (i // nt, 0, 0, i % nt))
    t_shape = jax.ShapeDtypeStruct((nb, ATT_HEADS, ATT_HD, s_len), F32)
    r_shape = jax.ShapeDtypeStruct((m, D_RET), F32)
    return pl.pallas_call(
        _inproj_kernel,
        grid=(m // tm,),
        in_specs=_inproj_specs(d, ws, wt, tm, nt),
        out_specs=[row_out, row_out, row_out, row_out, row_out, t_out, t_out, t_out],
        out_shape=[r_shape, r_shape, r_shape, r_shape, r_shape, t_shape, t_shape, t_shape],
        compiler_params=_params("arbitrary"),
        name="inproj",
    )(x, g, ws, wt, *tabs, qg, kg)


def _inproj_retention(x, g, ws, wt, tabs, qg, kg, rtabs, ng, nb, s_len, tm, layer, depth, kv_all=None):
    m, d = x.shape
    nt = s_len // tm
    decay, read, write, gl = rtabs
    first = kv_all is None
    row_out = pl.BlockSpec((tm, D_RET), lambda i: (i, 0))
    t_out = pl.BlockSpec((None, ATT_HEADS, ATT_HD, tm), lambda i: (i // nt, 0, 0, i % nt))
    if first:
        kv_out = pl.BlockSpec((depth, None, ATT_HEADS, ATT_HD, tm), lambda i: (0, i // nt, 0, 0, i % nt))
    else:
        kv_out = pl.BlockSpec((None, None, ATT_HEADS, ATT_HD, tm), lambda i: (layer, i // nt, 0, 0, i % nt))
    st_out = pl.BlockSpec((None, RET_HEADS, RET_DK, RET_DV), lambda i: (i // nt, 0, 0, 0))
    t_shape = jax.ShapeDtypeStruct((nb, ATT_HEADS, ATT_HD, s_len), F32)
    kv_shape = jax.ShapeDtypeStruct((depth, nb, ATT_HEADS, ATT_HD, s_len), F32)
    tile = pltpu.VMEM((tm, D_RET), F32)
    in_specs = _inproj_specs(d, ws, wt, tm, nt) + [_const_spec(t.shape) for t in (decay, read, write, gl, ng)]
    args = [x, g, ws, wt, *tabs, qg, kg, decay, read, write, gl, ng]
    aliases = {}
    if not first:
        aliases = {len(args): 3, len(args) + 1: 4}
        in_specs += [pl.BlockSpec(memory_space=pl.ANY)] * 2
        args += list(kv_all)
    return pl.pallas_call(
        functools.partial(_inproj_retention_kernel, tiles_per_seq=nt, first_layer=first),
        grid=(m // tm,),
        in_specs=in_specs,
        out_specs=[row_out, st_out, t_out, kv_out, kv_out, row_out, t_out],
        out_shape=[jax.ShapeDtypeStruct((m, D_RET), F32),
                   jax.ShapeDtypeStruct((nb, RET_HEADS, RET_DK, RET_DV), F32),
                   t_shape, kv_shape, kv_shape,
                   jax.ShapeDtypeStruct((m, D_ATT), BF16), jax.ShapeDtypeStruct(t_shape.shape, BF16)],
        scratch_shapes=[tile, tile, tile, tile, pltpu.VMEM((RET_HEADS, RET_DK, RET_DV), F32)],
        input_output_aliases=aliases,
        compiler_params=_params("arbitrary"),
        name="inproj_retention",
    )(*args)


def _retention_kernel(rq_ref, rk_ref, rv_ref, rg_ref, s0_ref, decay_ref, read_ref, write_ref, gl_ref, ng_ref,
                      y_ref, so_ref, st_ref):
    c = pl.program_id(1)

    @pl.when(c == 0)
    def _():
        st_ref[...] = s0_ref[...]

    _retention_chunks(rq_ref, rk_ref, rv_ref, rg_ref, decay_ref, read_ref, write_ref, gl_ref, ng_ref, y_ref, st_ref)

    @pl.when(c == pl.num_programs(1) - 1)
    def _():
        so_ref[...] = st_ref[...]


def _retention(rq, rk, rv, rg, state0, tabs, ng, nb, n_chunks, chunk):
    m = rq.shape[0]
    decay, read, write, gl = tabs
    per_step = next(c for c in (4, 2, 1) if n_chunks % c == 0)
    n_steps = n_chunks // per_step
    row = pl.BlockSpec((per_step * chunk, D_RET), lambda b, c: (b * n_steps + c, 0))
    st = pl.BlockSpec((None, RET_HEADS, RET_DK, RET_DV), lambda b, c: (b, 0, 0, 0))
    cs = lambda shape: pl.BlockSpec(shape, lambda b, c: (0,) * len(shape))
    return pl.pallas_call(
        _retention_kernel,
        grid=(nb, n_steps),
        in_specs=[row, row, row, row, st, cs(decay.shape), cs(read.shape), cs(write.shape), cs(gl.shape), cs(ng.shape)],
        out_specs=[row, st],
        out_shape=[jax.ShapeDtypeStruct((m, D_RET), F32), jax.ShapeDtypeStruct(state0.shape, F32)],
        scratch_shapes=[pltpu.VMEM((RET_HEADS, RET_DK, RET_DV), F32)],
        compiler_params=_params("arbitrary", "arbitrary"),
        name="retention",
    )(rq, rk, rv, rg, state0, decay, read, write, gl, ng)


def _select_blocks(gate, n_past):
    nb = gate.shape[0]
    blk = lax.broadcasted_iota(jnp.int32, gate.shape, 0)
    valid = blk < n_past
    gm = jnp.where(valid, gate, NEG_INF)
    rank = jnp.zeros(gate.shape, jnp.int32)
    for m in range(nb):
        row = gm[m:m + 1, :]
        ahead = (row > gm) | ((row == gm) & (blk > m))
        rank = rank + ahead.astype(jnp.int32)
    return valid & (rank < MOBA_TOPK) & (jnp.abs(gate) < jnp.inf)


def _moba_prompt_kernel(qt_ref, kt_ref, kb_ref, vt_ref, o_ref, km_ref, bias_ref, s_ref, acc_ref):
    qi = pl.program_id(1)
    nb = kt_ref.shape[-1] // MOBA_BLOCK
    n_pairs = ATT_HEADS // 2

    @pl.when(qi == 0)
    def _():
        for h in range(ATT_HEADS):
            for n in range(nb):
                kblk = kt_ref[h, :, n * MOBA_BLOCK:(n + 1) * MOBA_BLOCK]
                mean = jnp.sum(kblk, axis=-1, keepdims=True) * (1.0 / MOBA_BLOCK)
                km_ref[h, n] = jnp.broadcast_to(mean, (ATT_HD, Q_BLOCK))

    ob = (qi * Q_BLOCK) // MOBA_BLOCK
    q_off = qi * Q_BLOCK - ob * MOBA_BLOCK

    q_pairs = []
    zero = jnp.zeros((ATT_HD, Q_BLOCK), BF16)
    for h in range(ATT_HEADS):
        qt = qt_ref[h]
        gate = jnp.concatenate(
            [jnp.sum(qt * km_ref[h, n], axis=0, keepdims=True) for n in range(nb)], axis=0)
        bias_ref[h] = jnp.where(_select_blocks(gate, ob), 0.0, NEG_INF).astype(F32)
        qs = (qt * ATT_HD ** -0.5).astype(BF16)
        if h % 2 == 0:
            top = jnp.concatenate([qs, zero], axis=1)
        else:
            q_pairs.append(jnp.concatenate([top, jnp.concatenate([zero, qs], axis=1)], axis=0))

    def scores(n, pair):
        start = pl.multiple_of(n * MOBA_BLOCK, MOBA_BLOCK)
        kb = kb_ref[pl.ds(start, MOBA_BLOCK), pair * 2 * ATT_HD:(pair + 1) * 2 * ATT_HD]
        return jnp.dot(kb, q_pairs[pair], preferred_element_type=F32)

    def values(n, h):
        start = pl.multiple_of(n * MOBA_BLOCK, MOBA_BLOCK)
        return vt_ref[h, :, pl.ds(start, MOBA_BLOCK)]

    key = lax.broadcasted_iota(jnp.int32, (MOBA_BLOCK, Q_BLOCK), 0)
    qry = lax.broadcasted_iota(jnp.int32, (MOBA_BLOCK, Q_BLOCK), 1)
    causal = jnp.where(key <= qry + q_off, 0.0, NEG_INF).astype(F32)

    def score_pass(n, masks, m_prev):
        m_new = []
        pair_scores = [scores(n, pair) for pair in range(n_pairs)]
        for pair in range(n_pairs):
            st = pair_scores[pair]
            for j in range(2):
                h = 2 * pair + j
                tile = st[:, j * Q_BLOCK:(j + 1) * Q_BLOCK] + masks[h]
                s_ref[n, h] = tile
                m_new.append(jnp.maximum(m_prev[h], jnp.max(tile, axis=0, keepdims=True)))
        return m_new

    m_own = score_pass(ob, [causal] * ATT_HEADS, [jnp.full((1, Q_BLOCK), NEG_INF, F32)] * ATT_HEADS)

    def past_scores(n, m_prev):
        return tuple(score_pass(n, [bias_ref[h, pl.ds(n, 1), :] for h in range(ATT_HEADS)], list(m_prev)))

    m_fin = lax.fori_loop(0, ob, past_scores, tuple(m_own))

    ones_rows = jnp.ones((SUM_ROWS, MOBA_BLOCK), BF16)

    def value_pass(n, acc_prev):
        ps = [jnp.exp((s_ref[n, h] - m_fin[h]).astype(BF16)) for h in range(ATT_HEADS)]
        return [acc_prev[h] + jnp.dot(jnp.concatenate([values(n, h), ones_rows], axis=0), ps[h],
                                      preferred_element_type=F32) for h in range(ATT_HEADS)]

    acc_own = value_pass(ob, [jnp.zeros((ATT_HD + SUM_ROWS, Q_BLOCK), F32)] * ATT_HEADS)
    for h in range(ATT_HEADS):
        acc_ref[h] = acc_own[h]

    def past_values(n, _):
        acc_new = value_pass(n, [acc_ref[h] for h in range(ATT_HEADS)])
        for h in range(ATT_HEADS):
            acc_ref[h] = acc_new[h]
        return 0

    lax.fori_loop(0, ob, past_values, 0)
    out_t = jnp.concatenate([acc_ref[h, 0:ATT_HD, :] / acc_ref[h, ATT_HD:ATT_HD + 1, :] for h in range(ATT_HEADS)],
                            axis=0)
    o_ref[...] = out_t.T


def _moba_prompt(qt, kt_all, layer, kb, vtb, m_rows):
    _, nbat, _, _, s_len = kt_all.shape
    nqb = s_len // Q_BLOCK
    nblk = s_len // MOBA_BLOCK
    full = pl.BlockSpec((None, ATT_HEADS, ATT_HD, s_len), lambda b, i: (b, 0, 0, 0))
    return pl.pallas_call(
        _moba_prompt_kernel,
        grid=(nbat, nqb),
        in_specs=[
            pl.BlockSpec((None, ATT_HEADS, ATT_HD, Q_BLOCK), lambda b, i: (b, 0, 0, i)),
            pl.BlockSpec((None, None, ATT_HEADS, ATT_HD, s_len), lambda b, i: (layer, b, 0, 0, 0)),
            pl.BlockSpec((s_len, D_ATT), lambda b, i: (b, 0)),
            full,
        ],
        out_specs=pl.BlockSpec((Q_BLOCK, D_ATT), lambda b, i: (b * nqb + i, 0)),
        out_shape=jax.ShapeDtypeStruct((m_rows, D_ATT), F32),
        scratch_shapes=[pltpu.VMEM((ATT_HEADS, nblk, ATT_HD, Q_BLOCK), F32),
                        pltpu.VMEM((ATT_HEADS, nblk, Q_BLOCK), F32),
                        pltpu.VMEM((nblk, ATT_HEADS, MOBA_BLOCK, Q_BLOCK), F32),
                        pltpu.VMEM((ATT_HEADS, ATT_HD + SUM_ROWS, Q_BLOCK), F32)],
        compiler_params=_params("arbitrary", "arbitrary"),
        name="moba_prompt",
    )(qt, kt_all, kb, vtb)


def _kmean_kernel(pt_ref, *refs):
    del pt_ref
    page_refs, o_ref = refs[:-1], refs[-1]
    j = pl.program_id(2)
    n_per_step = len(page_refs) // PAGES_PER_BLOCK

    @pl.when(j == 0)
    def _():
        o_ref[...] = jnp.zeros(o_ref.shape, F32)

    lane = lax.broadcasted_iota(jnp.int32, o_ref.shape, 1)
    out = o_ref[...]
    for p in range(n_per_step):
        tot = page_refs[PAGES_PER_BLOCK * p][...]
        for r in range(1, PAGES_PER_BLOCK):
            tot = tot + page_refs[PAGES_PER_BLOCK * p + r][...]
        mean = jnp.sum(tot.reshape(D_ATT, PAGE_SIZE), axis=-1, keepdims=True) * (1.0 / MOBA_BLOCK)
        out = jnp.where(lane == j * n_per_step + p, mean, out)
    o_ref[...] = out


def _cache_kmean(cache_t, page_table, nbf, blocks_per_step):
    depth = cache_t.shape[0]
    db = page_table.shape[0]
    pages_per_step = blocks_per_step * PAGES_PER_BLOCK

    def page_spec(i):
        return pl.BlockSpec((None, None, ATT_HEADS, ATT_HD, PAGE_SIZE),
                            lambda l, b, j, pt: (l, pt[b, j * pages_per_step + i], 0, 0, 0))

    return pl.pallas_call(
        _kmean_kernel,
        grid_spec=pltpu.PrefetchScalarGridSpec(
            num_scalar_prefetch=1,
            grid=(depth, db, nbf // blocks_per_step),
            in_specs=[page_spec(i) for i in range(pages_per_step)],
            out_specs=pl.BlockSpec((None, None, D_ATT, nbf), lambda l, b, j, pt: (l, b, 0, 0)),
        ),
        out_shape=jax.ShapeDtypeStruct((depth, db, D_ATT, nbf), F32),
        compiler_params=_params("arbitrary", "arbitrary", "arbitrary"),
        name="cache_kmean",
    )(page_table, *([cache_t] * pages_per_step))


def _topk_kernel(q_ref, km_ref, idx_ref):
    nbf = km_ref.shape[-1]
    n_q = q_ref.shape[0]
    lane = lax.broadcasted_iota(jnp.int32, (n_q, nbf), 1).astype(F32)
    out_lane = lax.broadcasted_iota(jnp.int32, (n_q, 128), 1)
    for h in range(ATT_HEADS):
        q = q_ref[:, h * ATT_HD:(h + 1) * ATT_HD]
        gate = jnp.dot(q, km_ref[h * ATT_HD:(h + 1) * ATT_HD, :], precision=lax.Precision.HIGHEST,
                       preferred_element_type=F32)
        out = jnp.zeros((n_q, 128), jnp.int32)
        for t in range(MOBA_TOPK):
            best = jnp.max(gate, axis=-1, keepdims=True)
            idx = jnp.min(jnp.where(gate == best, lane, float(nbf)), axis=-1, keepdims=True)
            out = jnp.where(out_lane == t, idx.astype(jnp.int32), out)
            gate = jnp.where(lane == idx, NEG_INF, gate)
        idx_ref[h] = out


def _sample_topk(q, kmean_l):
    db, _, nbf = kmean_l.shape
    n_q = q.shape[0] // db
    return pl.pallas_call(
        _topk_kernel,
        grid=(db,),
        in_specs=[pl.BlockSpec((n_q, D_ATT), lambda b: (b, 0)),
                  pl.BlockSpec((None, D_ATT, nbf), lambda b: (b, 0, 0))],
        out_specs=pl.BlockSpec((None, ATT_HEADS, n_q, 128), lambda b: (b, 0, 0, 0)),
        out_shape=jax.ShapeDtypeStruct((db, ATT_HEADS, n_q, 128), jnp.int32),
        compiler_params=_params("arbitrary"),
        name="sample_topk",
    )(q, kmean_l)


def _moba_sample_kernel(gidx_ref, pt_ref, qt_ref, ktn_ref, vtn_ref, ck_hbm, cv_hbm, o_ref, kbuf, vbuf, sem, *, layer):
    n_hsteps = pl.num_programs(1)
    step = pl.program_id(0) * n_hsteps + pl.program_id(1)
    n_steps = pl.num_programs(0) * n_hsteps
    heads_per_step, _, n_q = qt_ref.shape
    n_sel = MOBA_TOPK * PAGES_PER_BLOCK
    picks_per_head = n_q * MOBA_TOPK
    scale = ATT_HD ** -0.5

    def for_each_copy(step_, slot_, fn):
        b_ = step_ // n_hsteps
        h0 = (step_ % n_hsteps) * heads_per_step
        base = step_ * (heads_per_step * picks_per_head)

        for hh in range(heads_per_step):
            def body(g, _, hh=hh):
                blk = gidx_ref[base + hh * picks_per_head + g]
                for r in range(PAGES_PER_BLOCK):
                    pid = pt_ref[b_, blk * PAGES_PER_BLOCK + r]
                    i = (hh * picks_per_head + g) * PAGES_PER_BLOCK + r
                    fn(pltpu.make_async_copy(ck_hbm.at[layer, pid, h0 + hh], kbuf.at[slot_, i], sem.at[slot_, 0]))
                    fn(pltpu.make_async_copy(cv_hbm.at[layer, pid, h0 + hh], vbuf.at[slot_, i], sem.at[slot_, 1]))
                return 0

            lax.fori_loop(0, picks_per_head, body, 0, unroll=4)

    def start_all(step_, slot_):
        for_each_copy(step_, slot_, lambda c: c.start())

    def wait_all(step_, slot_):
        for_each_copy(step_, slot_, lambda c: c.wait())

    slot = step % 2

    @pl.when(step == 0)
    def _():
        start_all(step, slot)

    @pl.when(step + 1 < n_steps)
    def _():
        start_all(step + 1, 1 - slot)

    wait_all(step, slot)

    def rows(fn):
        return jnp.concatenate([fn(l) for l in range(n_q)], axis=0)

    qry = lax.broadcasted_iota(jnp.int32, (n_q, n_q), 0)
    key = lax.broadcasted_iota(jnp.int32, (n_q, n_q), 1)
    for hh in range(heads_per_step):
        qt = qt_ref[hh]
        ktn = ktn_ref[hh]
        vtn = vtn_ref[hh]
        page0 = hh * n_q * n_sel
        q_cols = [qt[:, l:l + 1] for l in range(n_q)]
        q_wide = [jnp.broadcast_to(qc, (ATT_HD, PAGE_SIZE)) for qc in q_cols]
        s_new = rows(lambda l: jnp.sum(q_cols[l] * ktn, axis=0, keepdims=True)) * scale
        s_new = jnp.where(key <= qry, s_new, NEG_INF)
        s_sel = [rows(lambda l: jnp.sum(q_wide[l] * kbuf[slot, page0 + l * n_sel + j], axis=0, keepdims=True)) * scale
                 for j in range(n_sel)]
        m = jnp.max(s_new, axis=-1, keepdims=True)
        for s in s_sel:
            m = jnp.maximum(m, jnp.max(s, axis=-1, keepdims=True))
        p_new = jnp.exp(s_new - m)
        p_sel = [jnp.exp(s - m) for s in s_sel]
        denom = jnp.sum(p_new, axis=-1, keepdims=True)
        for p in p_sel:
            denom = denom + jnp.sum(p, axis=-1, keepdims=True)
        for l in range(n_q):
            acc = p_sel[0][l:l + 1, :] * vbuf[slot, page0 + l * n_sel]
            for j in range(1, n_sel):
                acc = acc + p_sel[j][l:l + 1, :] * vbuf[slot, page0 + l * n_sel + j]
            out = jnp.sum(acc, axis=-1, keepdims=True) + jnp.sum(p_new[l:l + 1, :] * vtn, axis=-1, keepdims=True)
            o_ref[hh, :, l:l + 1] = out / denom[l:l + 1, :]


def _moba_sample(gidx, page_table, qt, ktn, vtn, cache_k_t, cache_v_t, layer):
    db, _, _, n_q = qt.shape
    heads_per_step = 2
    n_pages = heads_per_step * n_q * MOBA_TOPK * PAGES_PER_BLOCK
    small = pl.BlockSpec((None, heads_per_step, ATT_HD, n_q), lambda b, h, gi, pt: (b, h, 0, 0))
    hbm = pl.BlockSpec(memory_space=pl.ANY)
    return pl.pallas_call(
        functools.partial(_moba_sample_kernel, layer=layer),
        grid_spec=pltpu.PrefetchScalarGridSpec(
            num_scalar_prefetch=2,
            grid=(db, ATT_HEADS // heads_per_step),
            in_specs=[small, small, small, hbm, hbm],
            out_specs=small,
            scratch_shapes=[pltpu.VMEM((2, n_pages, ATT_HD, PAGE_SIZE), F32),
                            pltpu.VMEM((2, n_pages, ATT_HD, PAGE_SIZE), F32),
                            pltpu.SemaphoreType.DMA((2, 2))],
        ),
        out_shape=jax.ShapeDtypeStruct(qt.shape, F32),
        compiler_params=_params("arbitrary", "arbitrary"),
        name="moba_sample",
    )(gidx, page_table, qt, ktn, vtn, cache_k_t, cache_v_t)


FF_CHUNK = 256
KMEAN_SLOTS = 3


def _mix_ffn_body(yr_ref, ao_ref, h_ref, wo_ref, g_ref, wg_ref, wu_ref, wd_ref, o_ref, before_chunk=None):
    h1 = (h_ref[...]
          + jnp.dot(yr_ref[...].astype(BF16), wo_ref[0:D_RET, :], preferred_element_type=F32)
          + jnp.dot(ao_ref[...].astype(BF16), wo_ref[D_RET:D_RET + D_ATT, :], preferred_element_type=F32))
    hn = (h1 * lax.rsqrt(jnp.mean(h1 * h1, axis=-1, keepdims=True) + RMS_EPS) * g_ref[...]).astype(BF16)
    acc = jnp.zeros_like(h1)
    d_ff = wg_ref.shape[1]
    for c in range(d_ff // FF_CHUNK):
        if before_chunk is not None:
            before_chunk(c)
        sl = slice(c * FF_CHUNK, (c + 1) * FF_CHUNK)
        gate = jnp.dot(hn, wg_ref[:, sl], preferred_element_type=F32)
        up = jnp.dot(hn, wu_ref[:, sl], preferred_element_type=F32)
        act = (gate * jax.nn.sigmoid(gate) * up).astype(BF16)
        acc = acc + jnp.dot(act, wd_ref[sl, :], preferred_element_type=F32)
    o_ref[...] = h1 + acc


def _mix_ffn_kernel(yr_ref, ao_ref, h_ref, wo_ref, g_ref, wg_ref, wu_ref, wd_ref, o_ref):
    _mix_ffn_body(yr_ref, ao_ref, h_ref, wo_ref, g_ref, wg_ref, wu_ref, wd_ref, o_ref)


def _mix_ffn_kmean_kernel(pt_ref, yr_ref, ao_ref, h_ref, wo_ref, g_ref, wg_ref, wu_ref, wd_ref, ck_hbm,
                          o_ref, km_ref, pbuf, sem, *, layer, groups_per_tile, chunk_stride):
    i = pl.program_id(0)
    n_groups = pl.num_programs(0) * groups_per_tile
    n_pages = pt_ref.shape[1]
    pages_per_group = pbuf.shape[1]
    blocks_per_group = pages_per_group // PAGES_PER_BLOCK
    tiles_per_seq = n_pages // (groups_per_tile * pages_per_group)
    t = i % tiles_per_seq
    ahead = KMEAN_SLOTS - 1

    def for_each_copy(group, fn):
        page0 = group * pages_per_group
        b_ = page0 // n_pages
        p0 = page0 % n_pages
        slot_ = group % KMEAN_SLOTS
        for r in range(pages_per_group):
            fn(pltpu.make_async_copy(ck_hbm.at[layer, pt_ref[b_, p0 + r]], pbuf.at[slot_, r], sem.at[slot_]))

    @pl.when(i == 0)
    def _():
        for g in range(ahead):
            @pl.when(g < n_groups)
            def _():
                for_each_copy(g, lambda c: c.start())

    @pl.when(t == 0)
    def _():
        km_ref[...] = jnp.zeros(km_ref.shape, F32)

    lane = lax.broadcasted_iota(jnp.int32, km_ref.shape, 1)

    def before_chunk(c):
        if c % chunk_stride != 0 or c // chunk_stride >= groups_per_tile:
            return
        q = c // chunk_stride
        group = i * groups_per_tile + q

        @pl.when(group + ahead < n_groups)
        def _():
            for_each_copy(group + ahead, lambda cp: cp.start())

        for_each_copy(group, lambda cp: cp.wait())
        slot = group % KMEAN_SLOTS
        out = km_ref[...]
        for p in range(blocks_per_group):
            tot = pbuf[slot, PAGES_PER_BLOCK * p]
            for r in range(1, PAGES_PER_BLOCK):
                tot = tot + pbuf[slot, PAGES_PER_BLOCK * p + r]
            mean = jnp.sum(tot.reshape(D_ATT, PAGE_SIZE), axis=-1, keepdims=True) * (1.0 / MOBA_BLOCK)
            out = jnp.where(lane == (t * groups_per_tile + q) * blocks_per_group + p, mean, out)
        km_ref[...] = out

    _mix_ffn_body(yr_ref, ao_ref, h_ref, wo_ref, g_ref, wg_ref, wu_ref, wd_ref, o_ref, before_chunk)


def _mix_ffn(yr, ao, h, wo, g, wg, wu, wd, tm, kmean_stream=None):
    m, d = h.shape
    row = lambda w: pl.BlockSpec((tm, w), lambda i, *_: (i, 0))
    in_specs = [row(D_RET), row(D_ATT), row(d), _const_spec(wo.shape), _const_spec((1, d)),
                _const_spec(wg.shape), _const_spec(wu.shape), _const_spec(wd.shape)]
    out_shape = jax.ShapeDtypeStruct((m, d), F32)
    if kmean_stream is None:
        return pl.pallas_call(
            _mix_ffn_kernel, grid=(m // tm,), in_specs=in_specs, out_specs=row(d), out_shape=out_shape,
            compiler_params=_params("arbitrary"), name="mix_ffn",
        )(yr, ao, h, wo, g, wg, wu, wd)
    cache_k_t, page_table, layer, groups_per_tile = kmean_stream
    db, n_pages = page_table.shape
    n_tiles = m // tm
    pages_per_group = db * n_pages // (n_tiles * groups_per_tile)
    tiles_per_seq = n_tiles // db
    nbf = n_pages // PAGES_PER_BLOCK
    n_chunks = wg.shape[1] // FF_CHUNK
    return pl.pallas_call(
        functools.partial(_mix_ffn_kmean_kernel, layer=layer, groups_per_tile=groups_per_tile,
                          chunk_stride=n_chunks // groups_per_tile),
        grid_spec=pltpu.PrefetchScalarGridSpec(
            num_scalar_prefetch=1,
            grid=(n_tiles,),
            in_specs=in_specs + [pl.BlockSpec(memory_space=pl.ANY)],
            out_specs=[row(d), pl.BlockSpec((None, D_ATT, nbf), lambda i, pt: (i // tiles_per_seq, 0, 0))],
            scratch_shapes=[pltpu.VMEM((KMEAN_SLOTS, pages_per_group, ATT_HEADS, ATT_HD, PAGE_SIZE), F32),
                            pltpu.SemaphoreType.DMA((KMEAN_SLOTS,))],
        ),
        out_shape=[out_shape, jax.ShapeDtypeStruct((db, D_ATT, nbf), F32)],
        compiler_params=_params("arbitrary"),
        name="mix_ffn_kmean",
    )(page_table, yr, ao, h, wo, g, wg, wu, wd, cache_k_t)


def _rope_tables(pos):
    ret_freq = 1.0 / (ROPE_THETA ** jnp.linspace(0.0, 1.0, RET_DK // 2, dtype=F32))
    att_freq = 1.0 / (ROPE_THETA ** (jnp.arange(0, ATT_HD, 2, dtype=F32) / ATT_HD))
    ang_r = pos.astype(F32)[:, None] * ret_freq[None, :]
    ang_a = pos.astype(F32)[:, None] * att_freq[None, :]
    rcos = jnp.concatenate([jnp.cos(ang_r), jnp.cos(ang_r)], axis=-1)
    rsin = jnp.concatenate([-jnp.sin(ang_r), jnp.sin(ang_r)], axis=-1)
    return rcos, rsin, jnp.cos(ang_a).T, jnp.sin(ang_a).T


def _retention_tables(chunk):
    log_gamma = jnp.log1p(-jnp.exp2(-5.0 - jnp.arange(RET_HEADS, dtype=F32)))
    i = jnp.arange(chunk, dtype=F32)
    diff = i[:, None] - i[None, :]
    decay = jnp.where(diff >= 0, jnp.exp(log_gamma[:, None, None] * jnp.maximum(diff, 0.0)), 0.0)
    read = jnp.exp(log_gamma[None, :] * (i[:, None] + 1.0))
    write = jnp.exp(log_gamma[None, :] * (chunk - 1.0 - i[:, None]))
    gl = jnp.exp(log_gamma * chunk)[None, :]
    wide = lambda t: jnp.repeat(t, RET_DV, axis=1)
    return decay, wide(read), wide(write), wide(gl)


def kernel(x_prompt, x_sample, cache_k, cache_v, state_ret, page_table, norm1_g, w_in, q_norm_g, k_norm_g,
           ret_norm_g, w_out, norm2_g, w_gate, w_up, w_down):
    nbat, s_len, d_model = x_prompt.shape
    db, n_q, _ = x_sample.shape
    depth = w_in.shape[0]
    n_pages = page_table.shape[1]
    past = n_pages * PAGE_SIZE
    nbf = past // MOBA_BLOCK
    assert n_pages % PAGES_PER_BLOCK == 0, "the new tokens must start a fresh MoBA block"
    assert nbf >= MOBA_TOPK and s_len % MOBA_BLOCK == 0 and s_len % RET_CHUNK == 0

    cache_k_t = jnp.transpose(cache_k, (0, 1, 3, 4, 2))
    cache_v_t = jnp.transpose(cache_v, (0, 1, 3, 4, 2))

    tabs_p = _rope_tables(jnp.arange(s_len))
    rc, rs, ac, as_ = _rope_tables(past + jnp.arange(n_q))
    tabs_s = (jnp.tile(rc, (db, 1)), jnp.tile(rs, (db, 1)), jnp.tile(ac, (1, db)), jnp.tile(as_, (1, db)))
    rtab_p = _retention_tables(RET_CHUNK)
    rtab_s = _retention_tables(n_q)

    hp = x_prompt.reshape(nbat * s_len, d_model)
    hs = x_sample.reshape(db * n_q, d_model)
    tm_p = 512 if s_len % 512 == 0 else RET_CHUNK

    n_tiles = nbat * s_len // tm_p
    pages_per_tile = db * n_pages // n_tiles if (db * n_pages) % n_tiles == 0 else 0
    groups_per_tile = next((g for g in (4, 2, 1) if pages_per_tile and pages_per_tile % (g * PAGES_PER_BLOCK) == 0), 0)
    stream_kmean = groups_per_tile > 0 and n_tiles % db == 0 and n_pages % pages_per_tile == 0
    if not stream_kmean:
        kmean = _cache_kmean(cache_k_t, page_table, nbf, 8 if nbf % 8 == 0 else 1)
    kv_all = None
    rp, kss, vss, rss = [], [], [], []
    d_split = 2 * RET_HEADS * RET_DK + 2 * D_RET
    for l in range(depth):
        ws = w_in[l, :, :d_split].astype(BF16)
        wt = w_in[l, :, d_split:].T.astype(BF16)
        g1 = norm1_g[l][None, :]
        g2 = norm2_g[l][None, :]
        qg = q_norm_g[l][:, None]
        kg = k_norm_g[l][:, None]
        ng = ret_norm_g[l].reshape(1, D_RET)
        wo = w_out[l].astype(BF16)
        wg = w_gate[l].astype(BF16)
        wu = w_up[l].astype(BF16)
        wd = w_down[l].astype(BF16)

        yr, rst, aqt, *kv_all, akb, avtb = _inproj_retention(hp, g1, ws, wt, tabs_p, qg, kg, rtab_p, ng, nbat, s_len,
                                                              tm_p, l, depth, kv_all)
        ao = _moba_prompt(aqt, kv_all[0], l, akb, avtb, nbat * s_len)
        if stream_kmean:
            hp, kmean_l = _mix_ffn(yr, ao, hp, wo, g2, wg, wu, wd, tm_p, (cache_k_t, page_table, l, groups_per_tile))
        else:
            hp, kmean_l = _mix_ffn(yr, ao, hp, wo, g2, wg, wu, wd, tm_p), kmean[l]
        rp.append(rst)

        m_s = db * n_q
        rq, rk, rv, rg, aq, aqt, akt, avt = _inproj(hs, g1, ws, wt, tabs_s, qg, kg, 1, m_s, m_s)
        yr, rst = _retention(rq, rk, rv, rg, state_ret[l], rtab_s, ng, db, 1, n_q)
        gidx = _sample_topk(aq, kmean_l)[..., :MOBA_TOPK].reshape(-1)
        per_seq = lambda t: t.reshape(ATT_HEADS, ATT_HD, db, n_q).transpose(2, 0, 1, 3)
        aot = _moba_sample(gidx, page_table, per_seq(aqt), per_seq(akt), per_seq(avt), cache_k_t, cache_v_t, l)
        ao = aot.transpose(0, 3, 1, 2).reshape(m_s, D_ATT)
        hs = _mix_ffn(yr, ao, hs, wo, g2, wg, wu, wd, m_s)
        rows = lambda t: t.reshape(D_ATT, db, n_q).transpose(1, 2, 0).reshape(db, n_q, ATT_HEADS, ATT_HD)
        kss.append(rows(akt))
        vss.append(rows(avt))
        rss.append(rst)

    seq_major = lambda t: jnp.transpose(t, (0, 1, 4, 2, 3))
    return (hp.reshape(nbat, s_len, d_model), hs.reshape(db, n_q, d_model), seq_major(kv_all[0]), seq_major(kv_all[1]),
            jnp.stack(rp), jnp.stack(kss), jnp.stack(vss), jnp.stack(rss))
```
